```python
import functools
import jax, jax.numpy as jnp
from jax import lax
import numpy as np

D_MODEL = 1024
BATCH = 8
SEQ = 4096
DEPTH = 1
DEC_BATCH = 128
DEC_SEQ = 1
PAST_LEN = 8192
PAGE_SIZE = 128

N_MEM = 256
FOX_HEADS = 8
FOX_HD = 64
RET_HEADS = 4
RET_DK = 64
RET_DV = 128
XA_HEADS = 4
XA_HD = 128
FOX_W = FOX_HEADS * FOX_HD
RET_QK_W = RET_HEADS * RET_DK
RET_V_W = RET_HEADS * RET_DV
XA_W = XA_HEADS * XA_HD
D_FF = -(-8 * D_MODEL // (3 * 256)) * 256
Q_BLOCK = 128
RET_CHUNK = 128
ROPE_BASE = 10000.0
EPS = 1e-6
NEG_INF = -1e30
SPLITS = (FOX_W, FOX_W, FOX_W, FOX_HEADS, RET_QK_W, RET_QK_W, RET_V_W, RET_V_W, XA_W, D_MODEL, D_MODEL, D_MODEL)
D_IN = 3 * FOX_W + FOX_HEADS + 2 * RET_QK_W + 2 * RET_V_W + XA_W + 3 * D_MODEL

kernel_name = 'fox_retnet_memxattn_hybrid_step'


def rmsnorm(x, g):
    xf = x.astype(jnp.float32)
    y = xf * lax.rsqrt(jnp.mean(xf * xf, axis=-1, keepdims=True) + EPS)
    return (y * g.astype(jnp.float32)).astype(x.dtype)


def rotary(x, pos):
    half = x.shape[-1] // 2
    inv = ROPE_BASE ** (-jnp.arange(half, dtype=jnp.float32) / half)
    ang = pos.astype(jnp.float32)[:, None] * inv[None, :]
    cos = jnp.cos(ang)[None, :, None, :]
    sin = jnp.sin(ang)[None, :, None, :]
    xf = x.astype(jnp.float32)
    x1, x2 = xf[..., :half], xf[..., half:]
    return jnp.concatenate([x1 * cos - x2 * sin, x1 * sin + x2 * cos], axis=-1).astype(x.dtype)


def fox_prompt(q, k, v, logf):
    b, s, h, d = q.shape
    nb = s // Q_BLOCK
    c = jnp.cumsum(logf, axis=1).transpose(0, 2, 1)
    qb = q.reshape(b, nb, Q_BLOCK, h, d).transpose(1, 0, 2, 3, 4)
    cb = c.reshape(b, h, nb, Q_BLOCK).transpose(2, 0, 1, 3)
    kpos = jnp.arange(s)
    scale = d ** -0.5

    def one_block(args):
        qi, ci, bi = args
        qpos = bi * Q_BLOCK + jnp.arange(Q_BLOCK)
        sc = jnp.einsum('bqhd,bkhd->bhqk', qi, k).astype(jnp.float32) * scale
        sc = sc + ci[:, :, :, None] - c[:, :, None, :]
        sc = jnp.where(kpos[None, :] <= qpos[:, None], sc, NEG_INF)
        p = jax.nn.softmax(sc, axis=-1).astype(v.dtype)
        return jnp.einsum('bhqk,bkhd->bqhd', p, v)

    out = lax.map(one_block, (qb, cb, jnp.arange(nb)))
    return out.transpose(1, 0, 2, 3, 4).reshape(b, s, h, d)


def fox_sample(q, k, v, logf, k_past, v_past, logf_past):
    t, d = q.shape[1], q.shape[-1]
    n_past = k_past.shape[1]
    scale = d ** -0.5
    lp = logf_past.astype(jnp.float32)
    r = (lax.cumsum(lp, axis=1, reverse=True) - lp).transpose(0, 2, 1)
    cn = jnp.cumsum(logf, axis=1).transpose(0, 2, 1)
    sp = jnp.einsum('bqhd,bkhd->bhqk', q, k_past).astype(jnp.float32) * scale
    sp = sp + cn[:, :, :, None] + r[:, :, None, :]
    sn = jnp.einsum('bqhd,bkhd->bhqk', q, k).astype(jnp.float32) * scale
    sn = sn + cn[:, :, :, None] - cn[:, :, None, :]
    causal = jnp.arange(t)[None, :] <= jnp.arange(t)[:, None]
    sn = jnp.where(causal, sn, NEG_INF)
    p = jax.nn.softmax(jnp.concatenate([sp, sn], axis=-1), axis=-1).astype(v.dtype)
    return (jnp.einsum('bhqk,bkhd->bqhd', p[..., :n_past], v_past)
            + jnp.einsum('bhqk,bkhd->bqhd', p[..., n_past:], v))


def retention(q, k, v, s0):
    b, l, h, dk = q.shape
    dv = v.shape[-1]
    c = RET_CHUNK if l % RET_CHUNK == 0 else l
    n = l // c
    log_g = jnp.log(1.0 - 2.0 ** (-5.0 - jnp.arange(h, dtype=jnp.float32)))
    i = jnp.arange(c, dtype=jnp.float32)
    diff = i[:, None] - i[None, :]
    dmask = jnp.where(diff[None] >= 0, jnp.exp(jnp.maximum(diff, 0.0)[None] * log_g[:, None, None]), 0.0)
    q_dec = jnp.exp((i + 1.0)[:, None] * log_g[None, :])
    k_dec = jnp.exp((c - 1.0 - i)[:, None] * log_g[None, :])
    chunk_dec = jnp.exp(c * log_g)
    qf = q.astype(jnp.float32).reshape(b, n, c, h, dk).transpose(1, 0, 2, 3, 4)
    kf = (k.astype(jnp.float32) * dk ** -0.5).reshape(b, n, c, h, dk).transpose(1, 0, 2, 3, 4)
    vf = v.astype(jnp.float32).reshape(b, n, c, h, dv).transpose(1, 0, 2, 3, 4)

    def step(s, xs):
        qc, kc, vc = xs
        inner = jnp.einsum('bihd,bjhd->bhij', qc, kc) * dmask[None]
        o = (jnp.einsum('bhij,bjhe->bihe', inner, vc)
             + jnp.einsum('bihd,bhde->bihe', qc * q_dec[None, :, :, None], s))
        s = chunk_dec[None, :, None, None] * s + jnp.einsum('bjhd,bjhe->bhde', kc * k_dec[None, :, :, None], vc)
        return s, o

    s_fin, o = lax.scan(step, s0.astype(jnp.float32), (qf, kf, vf))
    return o.transpose(1, 0, 2, 3, 4).reshape(b, l, h, dv), s_fin


def head_norm(o, g):
    mu = jnp.mean(o, axis=-1, keepdims=True)
    var = jnp.mean(jnp.square(o - mu), axis=-1, keepdims=True)
    y = ((o - mu) * lax.rsqrt(var + EPS)).reshape(o.shape[0], o.shape[1], -1)
    return y * g.astype(jnp.float32)


def memory_kv(mem, g_mem, w_mem_kv):
    b, m, _ = mem.shape
    kv = rmsnorm(mem, g_mem) @ w_mem_kv
    mk, mv = jnp.split(kv, 2, axis=-1)
    return mk.reshape(b, m, XA_HEADS, XA_HD), mv.reshape(b, m, XA_HEADS, XA_HD)


def cross_attn(q, mk, mv):
    sc = jnp.einsum('bqhd,bkhd->bhqk', q, mk).astype(jnp.float32) * (q.shape[-1] ** -0.5)
    p = jax.nn.softmax(sc, axis=-1).astype(mv.dtype)
    return jnp.einsum('bhqk,bkhd->bqhd', p, mv)


def layer(x, pos, fox_fn, ret_s0, mk, mv, g_attn, w_in, b_f, g_ret, w_br_fox, w_br_ret, w_br_xa,
          w_o, g_ffn, w_gu, w_down):
    b, l, _ = x.shape
    hn = rmsnorm(x, g_attn)
    z = hn @ w_in
    points = np.cumsum(np.array(SPLITS))[:-1].tolist()
    fq, fk, fv, flog, rq, rk, rv, rg, xq, ga, gb, gc = jnp.split(z, points, axis=-1)
    fq = fq.reshape(b, l, FOX_HEADS, FOX_HD)
    fk = fk.reshape(b, l, FOX_HEADS, FOX_HD)
    fv = fv.reshape(b, l, FOX_HEADS, FOX_HD)
    logf = jax.nn.log_sigmoid(flog.astype(jnp.float32) + b_f.astype(jnp.float32))
    o_fox = fox_fn(fq, fk, fv, logf).reshape(b, l, FOX_W)
    rq = rotary(rq.reshape(b, l, RET_HEADS, RET_DK), pos)
    rk = rotary(rk.reshape(b, l, RET_HEADS, RET_DK), pos)
    o_ret, s_new = retention(rq, rk, rv.reshape(b, l, RET_HEADS, RET_DV), ret_s0)
    o_ret = (jax.nn.silu(rg.astype(jnp.float32)) * head_norm(o_ret, g_ret)).astype(x.dtype)
    o_xa = cross_attn(xq.reshape(b, l, XA_HEADS, XA_HD), mk, mv).reshape(b, l, XA_W)
    mix = (jax.nn.sigmoid(ga) * (o_fox @ w_br_fox)
           + jax.nn.sigmoid(gb) * (o_ret @ w_br_ret)
           + jax.nn.sigmoid(gc) * (o_xa @ w_br_xa))
    x = x + mix @ w_o
    u_gate, u_up = jnp.split(rmsnorm(x, g_ffn) @ w_gu, 2, axis=-1)
    x = x + (jax.nn.silu(u_gate) * u_up) @ w_down
    return x, fk, fv, logf, s_new


def setup_inputs(seed: int = 0) -> dict:
    key = jax.random.key(seed)
    ks = jax.random.split(key, 24)
    n_pages = PAST_LEN // PAGE_SIZE
    n_used = DEC_BATCH * n_pages
    n_pool = n_used + (n_used + 3) // 4
    nrm = lambda k, shape, s: jax.random.normal(k, shape, jnp.float32) * s
    gain = lambda k, shape: 1.0 + 0.05 * jax.random.normal(k, shape, jnp.float32)
    page_table = jax.random.permutation(ks[0], n_pool)[:n_used].reshape(DEC_BATCH, n_pages).astype(jnp.int32)
    return {
        'x_prompt': nrm(ks[1], (BATCH, SEQ, D_MODEL), 1.0),
        'x_sample': nrm(ks[2], (DEC_BATCH, DEC_SEQ, D_MODEL), 1.0),
        'mem_prompt': nrm(ks[3], (BATCH, N_MEM, D_MODEL), 1.0),
        'cache_fox_k': nrm(ks[4], (DEPTH, n_pool, PAGE_SIZE, FOX_HEADS, FOX_HD), 1.0),
        'cache_fox_v': nrm(ks[5], (DEPTH, n_pool, PAGE_SIZE, FOX_HEADS, FOX_HD), 1.0),
        'cache_fox_logf': jax.nn.log_sigmoid(4.0 + nrm(ks[6], (DEPTH, n_pool, PAGE_SIZE, FOX_HEADS), 0.5)),
        'state_ret': nrm(ks[7], (DEPTH, DEC_BATCH, RET_HEADS, RET_DK, RET_DV), 2.0),
        'cache_mem_k': nrm(ks[8], (DEPTH, DEC_BATCH, N_MEM, XA_HEADS, XA_HD), 1.0),
        'cache_mem_v': nrm(ks[9], (DEPTH, DEC_BATCH, N_MEM, XA_HEADS, XA_HD), 1.0),
        'page_table': page_table,
        'g_attn': gain(ks[10], (DEPTH, D_MODEL)),
        'w_in': nrm(ks[11], (DEPTH, D_MODEL, D_IN), D_MODEL ** -0.5),
        'b_f': 4.0 + nrm(ks[12], (DEPTH, FOX_HEADS), 0.5),
        'g_ret': gain(ks[13], (DEPTH, RET_V_W)),
        'w_br_fox': nrm(ks[14], (DEPTH, FOX_W, D_MODEL), FOX_W ** -0.5),
        'w_br_ret': nrm(ks[15], (DEPTH, RET_V_W, D_MODEL), RET_V_W ** -0.5),
        'w_br_xa': nrm(ks[16], (DEPTH, XA_W, D_MODEL), XA_W ** -0.5),
        'w_o': nrm(ks[17], (DEPTH, D_MODEL, D_MODEL), D_MODEL ** -0.5),
        'g_ffn': gain(ks[18], (DEPTH, D_MODEL)),
        'w_gu': nrm(ks[19], (DEPTH, D_MODEL, 2 * D_FF), D_MODEL ** -0.5),
        'w_down': nrm(ks[20], (DEPTH, D_FF, D_MODEL), D_FF ** -0.5),
        'g_mem': gain(ks[21], (DEPTH, D_MODEL)),
        'w_mem_kv': nrm(ks[22], (DEPTH, D_MODEL, 2 * XA_W), D_MODEL ** -0.5),
        'g_final': gain(ks[23], (D_MODEL,)),
    }


def reference(x_prompt, x_sample, mem_prompt, cache_fox_k, cache_fox_v, cache_fox_logf, state_ret,
              cache_mem_k, cache_mem_v, page_table, g_attn, w_in, b_f, g_ret, w_br_fox, w_br_ret,
              w_br_xa, w_o, g_ffn, w_gu, w_down, g_mem, w_mem_kv, g_final):
    bp, sp_len = x_prompt.shape[0], x_prompt.shape[1]
    bs, t_len = x_sample.shape[0], x_sample.shape[1]
    past = page_table.shape[1] * cache_fox_k.shape[2]
    pos_p = jnp.arange(sp_len)
    pos_s = past + jnp.arange(t_len)
    hp, hs = x_prompt, x_sample
    p_k, p_v, p_lf, p_st, p_mk, p_mv = [], [], [], [], [], []
    s_k, s_v, s_lf, s_st = [], [], [], []
    for l in range(DEPTH):
        w_l = (g_attn[l], w_in[l], b_f[l], g_ret[l], w_br_fox[l], w_br_ret[l], w_br_xa[l],
               w_o[l], g_ffn[l], w_gu[l], w_down[l])
        mk_p, mv_p = memory_kv(mem_prompt, g_mem[l], w_mem_kv[l])
        s0 = jnp.zeros((bp, RET_HEADS, RET_DK, RET_DV), jnp.float32)
        hp, kp, vp, lfp, stp = layer(hp, pos_p, fox_prompt, s0, mk_p, mv_p, *w_l)
        k_past = cache_fox_k[l][page_table].reshape(bs, past, FOX_HEADS, FOX_HD)
        v_past = cache_fox_v[l][page_table].reshape(bs, past, FOX_HEADS, FOX_HD)
        lf_past = cache_fox_logf[l][page_table].reshape(bs, past, FOX_HEADS)
        fox_fn = functools.partial(fox_sample, k_past=k_past, v_past=v_past, logf_past=lf_past)
        hs, ks_, vs_, lfs, sts = layer(hs, pos_s, fox_fn, state_ret[l], cache_mem_k[l], cache_mem_v[l], *w_l)
        p_k.append(kp); p_v.append(vp); p_lf.append(lfp); p_st.append(stp); p_mk.append(mk_p); p_mv.append(mv_p)
        s_k.append(ks_); s_v.append(vs_); s_lf.append(lfs); s_st.append(sts)
    y_prompt = rmsnorm(hp, g_final)
    y_sample = rmsnorm(hs, g_final)
    return (y_prompt, y_sample,
            jnp.stack(p_k), jnp.stack(p_v), jnp.stack(p_lf), jnp.stack(p_st), jnp.stack(p_mk), jnp.stack(p_mv),
            jnp.stack(s_k), jnp.stack(s_v), jnp.stack(s_lf), jnp.stack(s_st))
```

```python
import functools

import jax
import jax.numpy as jnp
from jax import lax
from jax.experimental import pallas as pl
from jax.experimental.pallas import tpu as pltpu

D_MODEL = 1024
FOX_HEADS = 8
FOX_HD = 64
RET_HEADS = 4
RET_DK = 64
RET_DV = 128
XA_HEADS = 4
XA_HD = 128
FOX_W = FOX_HEADS * FOX_HD
RET_QK_W = RET_HEADS * RET_DK
RET_V_W = RET_HEADS * RET_DV
XA_W = XA_HEADS * XA_HD
D_FF = 2816
RET_CHUNK = 128
ROPE_BASE = 10000.0
EPS = 1e-6
NEG_INF = -1e30

LANES = 128
VMEM_LIMIT = 56 * 1024 * 1024

F32 = jnp.float32
BF16 = jnp.bfloat16

_C_FQ, _C_FK, _C_FV = 0, 512, 1024
_C_RQK, _C_RV, _C_RG, _C_XQ, _C_GATES = 1536, 2048, 2560, 3072, 3584
_W_MAIN = 6656


def _params(sem):
    return pltpu.CompilerParams(dimension_semantics=sem, vmem_limit_bytes=VMEM_LIMIT)


def _dot(a, b):
    return jnp.dot(a, b, preferred_element_type=F32)


def _dot_nt(a, b):
    return lax.dot_general(a, b, (((1,), (1,)), ((), ())), preferred_element_type=F32)


def _dot_tn(a, b):
    return lax.dot_general(a, b, (((0,), (0,)), ((), ())), preferred_element_type=F32)


def _rms(x, g):
    return x * lax.rsqrt(jnp.mean(x * x, axis=-1, keepdims=True) + EPS) * g


def _log_sigmoid(x):
    return -(jnp.maximum(-x, 0.0) + jnp.log1p(jnp.exp(-jnp.abs(x))))


def _sigmoid(x):
    return 1.0 / (1.0 + jnp.exp(-x))


def _split3(x):
    hi = x.astype(BF16)
    r1 = x - hi.astype(F32)
    mid = r1.astype(BF16)
    lo = (r1 - mid.astype(F32)).astype(BF16)
    return hi, mid, lo


def _expand_heads(x, n_heads, head_w):
    width = n_heads * head_w
    lane = lax.broadcasted_iota(jnp.int32, (1, width), 1)
    out = jnp.zeros((1, width), F32)
    for h in range(n_heads):
        sel = (lane >= h * head_w) & (lane < (h + 1) * head_w)
        out = jnp.where(sel, x[:, h:h + 1], out)
    return out


def _inproj_kernel(x_ref, g_ref, wm_ref, wfl_ref, wflt_ref, bf_ref, bft_ref, cos_ref, s1_ref, s2_ref,
                   gates_ref, fq_ref, fk_ref, fv_ref, fkb_ref, fvb_ref, logf_ref, ct_ref,
                   rq_ref, rk_ref, rv_ref, rg_ref, xq_ref, carry_ref, *, tm):
    j = pl.program_id(1)
    hn = _rms(x_ref[0], g_ref[...]).astype(BF16)

    def proj(c0, width):
        return _dot(hn, wm_ref[:, c0:c0 + width])

    fq_ref[0] = (proj(_C_FQ, FOX_W) * (FOX_HD ** -0.5)).astype(BF16)
    fk = proj(_C_FK, FOX_W)
    fk_ref[0] = fk
    fkb_ref[0] = fk.astype(BF16)
    fv = proj(_C_FV, FOX_W)
    fv_ref[0] = fv
    fvb_ref[0] = fv.astype(BF16)

    rqk = proj(_C_RQK, 2 * RET_QK_W)
    cos, s1, s2 = cos_ref[...], s1_ref[...], s2_ref[...]
    for ref, off in ((rq_ref, 0), (rk_ref, RET_QK_W)):
        r = rqk[:, off:off + RET_QK_W]
        ref[0] = (r * cos + pltpu.roll(r, RET_QK_W - RET_DK // 2, 1) * s1
                  + pltpu.roll(r, RET_DK // 2, 1) * s2)
    rv_ref[0] = proj(_C_RV, RET_V_W)
    rg_ref[0] = proj(_C_RG, RET_V_W)
    xq_ref[0] = proj(_C_XQ, XA_W)
    for c in range(0, 3 * D_MODEL, 512):
        gates_ref[0, :, c:c + 512] = proj(_C_GATES + c, 512)

    logf_ref[0] = _log_sigmoid(_dot(hn, wfl_ref[...]) + bf_ref[...])
    lft = _log_sigmoid(_dot_nt(wflt_ref[...], hn) + bft_ref[...])

    @pl.when(j == 0)
    def _():
        carry_ref[...] = jnp.zeros_like(carry_ref)

    row = lax.broadcasted_iota(jnp.int32, (tm, tm), 0)
    col = lax.broadcasted_iota(jnp.int32, (tm, tm), 1)
    tri = (row <= col).astype(BF16)
    hi, mid, lo = _split3(lft)
    ct = _dot(hi, tri) + _dot(mid, tri) + _dot(lo, tri) + carry_ref[:, 0:1]
    ct_ref[0] = ct
    carry_ref[...] = jnp.broadcast_to(ct[:, tm - 1:tm], carry_ref.shape)


def _inproj(x, g_attn, w_main, w_fl, w_flt, b_f, cos, s1, s2, tm):
    b, s, _ = x.shape
    grid = (b, s // tm)
    row = lambda w: pl.BlockSpec((1, tm, w), lambda i, j: (i, j, 0))
    const = lambda shape: pl.BlockSpec(shape, lambda i, j: (0,) * len(shape))
    tab = pl.BlockSpec((tm, RET_QK_W), lambda i, j: (j, 0))
    out_shapes = [
        ((b, s, 3 * D_MODEL), F32), ((b, s, FOX_W), BF16), ((b, s, FOX_W), F32), ((b, s, FOX_W), F32),
        ((b, s, FOX_W), BF16), ((b, s, FOX_W), BF16), ((b, s, FOX_HEADS), F32), ((b, FOX_HEADS, s), F32),
        ((b, s, RET_QK_W), F32), ((b, s, RET_QK_W), F32), ((b, s, RET_V_W), F32), ((b, s, RET_V_W), F32),
        ((b, s, XA_W), F32),
    ]
    out_specs = [row(sh[2]) for sh, _ in out_shapes]
    out_specs[7] = pl.BlockSpec((1, FOX_HEADS, tm), lambda i, j: (i, 0, j))
    return pl.pallas_call(
        functools.partial(_inproj_kernel, tm=tm),
        grid=grid,
        in_specs=[row(D_MODEL), const((1, D_MODEL)), const((D_MODEL, _W_MAIN)), const((D_MODEL, FOX_HEADS)),
                  const((FOX_HEADS, D_MODEL)), const((1, FOX_HEADS)), const((FOX_HEADS, 1)), tab, tab, tab],
        out_specs=out_specs,
        out_shape=[jax.ShapeDtypeStruct(sh, dt) for sh, dt in out_shapes],
        scratch_shapes=[pltpu.VMEM((FOX_HEADS, LANES), F32)],
        compiler_params=_params(("arbitrary", "arbitrary")),
        name="inproj",
    )(x, g_attn.reshape(1, D_MODEL), w_main, w_fl, w_flt, b_f.reshape(1, FOX_HEADS),
      b_f.reshape(FOX_HEADS, 1), cos, s1, s2)


def _fox_prompt_kernel(q_ref, k_ref, v_ref, ct_ref, o_ref, *, t):
    i = pl.program_id(2)
    lane = lax.broadcasted_iota(jnp.int32, (t, LANES), 1)
    q = q_ref[0]
    row = lax.broadcasted_iota(jnp.int32, (t, t), 0)
    col = lax.broadcasted_iota(jnp.int32, (t, t), 1)
    outs = []
    for hh in range(2):
        head_lanes = (lane >= hh * FOX_HD) & (lane < (hh + 1) * FOX_HD)
        qm = jnp.where(head_lanes, q, jnp.zeros_like(q))

        def tile(j, carry, diag):
            m, l, acc = carry
            off = pl.multiple_of(j * t, t)
            s = _dot_nt(qm, k_ref[0, pl.ds(off, t), :]) - ct_ref[0, 0, hh:hh + 1, pl.ds(off, t)]
            if diag:
                s = jnp.where(col <= row, s, NEG_INF)
            m_new = jnp.maximum(m, jnp.max(s, axis=1, keepdims=True))
            alpha = jnp.exp(m - m_new)
            p = jnp.exp(s - m_new)
            l = alpha * l + jnp.sum(p, axis=1, keepdims=True)
            acc = alpha * acc + _dot(p.astype(BF16), v_ref[0, pl.ds(off, t), :])
            return m_new, l, acc

        init = (jnp.full((t, 1), NEG_INF, F32), jnp.zeros((t, 1), F32), jnp.zeros((t, LANES), F32))
        carry = lax.fori_loop(0, i, functools.partial(tile, diag=False), init)
        _, l, acc = tile(i, carry, True)
        outs.append(acc / l)
    o_ref[0] = jnp.where(lane < FOX_HD, outs[0], outs[1])


def _fox_prompt(fqb, fkb, fvb, ct, t):
    b, s, _ = fqb.shape
    pairs = FOX_HEADS // 2
    ct4 = ct.reshape(b, pairs, 2, s)
    return pl.pallas_call(
        functools.partial(_fox_prompt_kernel, t=t),
        grid=(b, pairs, s // t),
        in_specs=[pl.BlockSpec((1, t, LANES), lambda bi, pr, i: (bi, i, pr)),
                  pl.BlockSpec((1, s, LANES), lambda bi, pr, i: (bi, 0, pr)),
                  pl.BlockSpec((1, s, LANES), lambda bi, pr, i: (bi, 0, pr)),
                  pl.BlockSpec((1, 1, 2, s), lambda bi, pr, i: (bi, pr, 0, 0))],
        out_specs=pl.BlockSpec((1, t, LANES), lambda bi, pr, i: (bi, i, pr)),
        out_shape=jax.ShapeDtypeStruct((b, s, FOX_W), F32),
        compiler_params=_params(("arbitrary", "arbitrary", "arbitrary")),
        name="fox_prompt",
    )(fqb, fkb, fvb, ct4)


def _ret_prompt_kernel(q_ref, k_ref, v_ref, rg_ref, gr_ref, dmask_ref, qdec_ref, kdec_ref, cdec_ref,
                       o_ref, st_ref):
    n = pl.program_id(2)

    @pl.when(n == 0)
    def _():
        st_ref[...] = jnp.zeros_like(st_ref)

    c = RET_CHUNK
    lane = lax.broadcasted_iota(jnp.int32, (c, LANES), 1)
    q = q_ref[0]
    k = k_ref[0] * (RET_DK ** -0.5)
    state = st_ref[0, 0]
    qd = (q * qdec_ref[0]).astype(BF16)
    kd = k * kdec_ref[0]
    kb = k.astype(BF16)
    new_state = cdec_ref[0] * state
    state_b = state.astype(BF16)
    for hh in range(2):
        head_lanes = (lane >= hh * RET_DK) & (lane < (hh + 1) * RET_DK)
        v = v_ref[0, :, hh * RET_DV:(hh + 1) * RET_DV]
        vb = v.astype(BF16)
        qm = jnp.where(head_lanes, q, 0.0).astype(BF16)
        inner = _dot_nt(qm, kb) * dmask_ref[0, hh]
        qdm = jnp.where(head_lanes, qd, jnp.zeros_like(qd))
        o = _dot(inner.astype(BF16), vb) + _dot(qdm, state_b)
        kdm = jnp.where(head_lanes, kd, 0.0).astype(BF16)
        new_state = new_state + _dot_tn(kdm, vb)
        mu = jnp.mean(o, axis=-1, keepdims=True)
        var = jnp.mean(jnp.square(o - mu), axis=-1, keepdims=True)
        y = (o - mu) * lax.rsqrt(var + EPS) * gr_ref[:, hh * RET_DV:(hh + 1) * RET_DV]
        rg = rg_ref[0, :, hh * RET_DV:(hh + 1) * RET_DV]
        o_ref[0, :, hh * RET_DV:(hh + 1) * RET_DV] = rg * _sigmoid(rg) * y
    st_ref[0, 0] = new_state


def _ret_tables(length):
    h = RET_HEADS
    log_g = jnp.log(1.0 - 2.0 ** (-5.0 - jnp.arange(h, dtype=F32)))
    i = jnp.arange(length, dtype=F32)
    diff = i[:, None] - i[None, :]
    dmask = jnp.where(diff[None] >= 0, jnp.exp(jnp.maximum(diff, 0.0)[None] * log_g[:, None, None]), 0.0)
    q_dec = jnp.exp((i + 1.0)[:, None] * log_g[None, :])
    k_dec = jnp.exp((length - 1.0 - i)[:, None] * log_g[None, :])
    chunk_dec = jnp.exp(length * log_g)
    return dmask, q_dec, k_dec, chunk_dec


def _ret_prompt(rq, rk, rv, rg, g_ret):
    b, s, _ = rq.shape
    c = RET_CHUNK
    pairs = RET_HEADS // 2
    dmask, q_dec, k_dec, chunk_dec = _ret_tables(c)
    dmask = dmask.reshape(pairs, 2, c, c)
    qdec = jnp.repeat(q_dec, RET_DK, axis=1).reshape(c, pairs, LANES).transpose(1, 0, 2)
    kdec = jnp.repeat(k_dec, RET_DK, axis=1).reshape(c, pairs, LANES).transpose(1, 0, 2)
    cdec = jnp.repeat(chunk_dec, RET_DK).reshape(pairs, LANES, 1)
    o, st = pl.pallas_call(
        _ret_prompt_kernel,
        grid=(b, pairs, s // c),
        in_specs=[pl.BlockSpec((1, c, LANES), lambda bi, pr, n: (bi, n, pr)),
                  pl.BlockSpec((1, c, LANES), lambda bi, pr, n: (bi, n, pr)),
                  pl.BlockSpec((1, c, 2 * RET_DV), lambda bi, pr, n: (bi, n, pr)),
                  pl.BlockSpec((1, c, 2 * RET_DV), lambda bi, pr, n: (bi, n, pr)),
                  pl.BlockSpec((1, 2 * RET_DV), lambda bi, pr, n: (0, pr)),
                  pl.BlockSpec((1, 2, c, c), lambda bi, pr, n: (pr, 0, 0, 0)),
                  pl.BlockSpec((1, c, LANES), lambda bi, pr, n: (pr, 0, 0)),
                  pl.BlockSpec((1, c, LANES), lambda bi, pr, n: (pr, 0, 0)),
                  pl.BlockSpec((1, LANES, 1), lambda bi, pr, n: (pr, 0, 0))],
        out_specs=[pl.BlockSpec((1, c, 2 * RET_DV), lambda bi, pr, n: (bi, n, pr)),
                   pl.BlockSpec((1, 1, 2 * RET_DK, RET_DV), lambda bi, pr, n: (bi, pr, 0, 0))],
        out_shape=[jax.ShapeDtypeStruct((b, s, RET_V_W), F32),
                   jax.ShapeDtypeStruct((b, pairs, 2 * RET_DK, RET_DV), F32)],
        compiler_params=_params(("arbitrary", "arbitrary", "arbitrary")),
        name="ret_prompt",
    )(rq, rk, rv, rg, g_ret.reshape(1, RET_V_W), dmask, qdec, kdec, cdec)
    return o, st.reshape(b, RET_HEADS, RET_DK, RET_DV)


def _ret_sample_kernel(q_ref, k_ref, v_ref, rg_ref, gr_ref, qdec_ref, cdec_ref, s0_ref, o_ref, s1_ref, *, bt):
    rows = RET_HEADS * RET_DK
    q = q_ref[...]
    k = k_ref[...] * (RET_DK ** -0.5)
    qg = q * qdec_ref[...]
    qk = q * k
    for h in range(RET_HEADS):
        v = v_ref[:, h * RET_DV:(h + 1) * RET_DV]
        cd = cdec_ref[:, h * RET_DV:(h + 1) * RET_DV]
        inner = jnp.sum(qk[:, h * RET_DK:(h + 1) * RET_DK], axis=1, keepdims=True)
        o = inner * v
        for d in range(RET_DK):
            r = h * RET_DK + d
            srow = s0_ref[pl.ds(r, bt, stride=rows), :]
            o = o + qg[:, r:r + 1] * srow
            s1_ref[pl.ds(r, bt, stride=rows), :] = cd * srow + k[:, r:r + 1] * v
        mu = jnp.mean(o, axis=-1, keepdims=True)
        var = jnp.mean(jnp.square(o - mu), axis=-1, keepdims=True)
        y = (o - mu) * lax.rsqrt(var + EPS) * gr_ref[:, h * RET_DV:(h + 1) * RET_DV]
        rg = rg_ref[:, h * RET_DV:(h + 1) * RET_DV]
        o_ref[:, h * RET_DV:(h + 1) * RET_DV] = rg * _sigmoid(rg) * y


def _ret_sample(rq, rk, rv, rg, g_ret, s0, bt):
    t = rq.shape[0]
    rows = RET_HEADS * RET_DK
    _, q_dec, _, chunk_dec = _ret_tables(1)
    qdec = jnp.repeat(q_dec, RET_DK, axis=1).reshape(1, RET_QK_W)
    cdec = jnp.repeat(chunk_dec, RET_DV).reshape(1, RET_V_W)
    tok = lambda w: pl.BlockSpec((bt, w), lambda i: (i, 0))
    const = lambda w: pl.BlockSpec((1, w), lambda i: (0, 0))
    st = pl.BlockSpec((bt * rows, RET_DV), lambda i: (i, 0))
    o, s1 = pl.pallas_call(
        functools.partial(_ret_sample_kernel, bt=bt),
        grid=(t // bt,),
        in_specs=[tok(RET_QK_W), tok(RET_QK_W), tok(RET_V_W), tok(RET_V_W), const(RET_V_W),
                  const(RET_QK_W), const(RET_V_W), st],
        out_specs=[tok(RET_V_W), st],
        out_shape=[jax.ShapeDtypeStruct((t, RET_V_W), F32), jax.ShapeDtypeStruct((t * rows, RET_DV), F32)],
        compiler_params=_params(("arbitrary",)),
        name="ret_sample",
    )(rq, rk, rv, rg, g_ret.reshape(1, RET_V_W), qdec, cdec, s0.reshape(t * rows, RET_DV))
    return o, s1.reshape(t, RET_HEADS, RET_DK, RET_DV)


def _xattn_prompt_kernel(q_ref, mk_ref, mv_ref, o_ref):
    scale = XA_HD ** -0.5
    for h in range(XA_HEADS):
        sl = slice(h * XA_HD, (h + 1) * XA_HD)
        s = _dot_nt(q_ref[0, :, sl].astype(BF16), mk_ref[0, :, sl].astype(BF16)) * scale
        p = jnp.exp(s - jnp.max(s, axis=1, keepdims=True))
        p = p / jnp.sum(p, axis=1, keepdims=True)
        o_ref[0, :, sl] = _dot(p.astype(BF16), mv_ref[0, :, sl].astype(BF16))


def _xattn_prompt(xq, mk, mv, t):
    b, s, _ = xq.shape
    m = mk.shape[1]
    return pl.pallas_call(
        _xattn_prompt_kernel,
        grid=(b, s // t),
        in_specs=[pl.BlockSpec((1, t, XA_W), lambda bi, i: (bi, i, 0)),
                  pl.BlockSpec((1, m, XA_W), lambda bi, i: (bi, 0, 0)),
                  pl.BlockSpec((1, m, XA_W), lambda bi, i: (bi, 0, 0))],
        out_specs=pl.BlockSpec((1, t, XA_W), lambda bi, i: (bi, i, 0)),
        out_shape=jax.ShapeDtypeStruct((b, s, XA_W), F32),
        compiler_params=_params(("arbitrary", "arbitrary")),
        name="xattn_prompt",
    )(xq, mk, mv)


def _head_query_block(q_row, n_cols, head_w):
    w = q_row.shape[1]
    r = lax.broadcasted_iota(jnp.int32, (w, w), 0)
    c = lax.broadcasted_iota(jnp.int32, (w, w), 1)
    diag = jnp.where(r == c, jnp.broadcast_to(q_row, (w, w)), 0.0)
    rr = lax.broadcasted_iota(jnp.int32, (w, n_cols), 0)
    cc = lax.broadcasted_iota(jnp.int32, (w, n_cols), 1)
    ones = ((rr >= cc * head_w) & (rr < (cc + 1) * head_w)).astype(F32)
    return _dot(diag, ones)


def _head_expand_matrix(n_rows, n_heads, head_w):
    width = n_heads * head_w
    rr = lax.broadcasted_iota(jnp.int32, (n_rows, width), 0)
    cc = lax.broadcasted_iota(jnp.int32, (n_rows, width), 1)
    return ((cc >= rr * head_w) & (cc < (rr + 1) * head_w)).astype(F32)


_HPAD = 8


def _xattn_sample_kernel(q_ref, mk_ref, mv_ref, o_ref):
    qblk = _head_query_block(q_ref[0], _HPAD, XA_HD)
    s = _dot(mk_ref[0], qblk) * (XA_HD ** -0.5)
    m = jnp.max(s, axis=0, keepdims=True)
    p = jnp.exp(s - m)
    l = jnp.sum(p, axis=0, keepdims=True)
    pe = _dot(p, _head_expand_matrix(_HPAD, XA_HEADS, XA_HD))
    acc = jnp.sum(pe * mv_ref[0], axis=0, keepdims=True)
    o_ref[0] = acc / _expand_heads(l, XA_HEADS, XA_HD)


def _xattn_sample(xq, mk, mv):
    t, m, _ = mk.shape
    o = pl.pallas_call(
        _xattn_sample_kernel,
        grid=(t,),
        in_specs=[pl.BlockSpec((1, 1, XA_W), lambda i: (i, 0, 0)),
                  pl.BlockSpec((1, m, XA_W), lambda i: (i, 0, 0)),
                  pl.BlockSpec((1, m, XA_W), lambda i: (i, 0, 0))],
        out_specs=pl.BlockSpec((1, 1, XA_W), lambda i: (i, 0, 0)),
        out_shape=jax.ShapeDtypeStruct((t, 1, XA_W), F32),
        compiler_params=_params(("arbitrary",)),
        name="xattn_sample",
    )(xq.reshape(t, 1, XA_W), mk, mv)
    return o.reshape(t, XA_W)


def _fox_sample_kernel(pt_ref, q_ref, cn_ref, kn_ref, vn_ref, *rest, pps, page, n_steps):
    k_refs = rest[:pps]
    lf_refs = rest[pps:2 * pps]
    v_refs = rest[2 * pps:3 * pps]
    o_ref = rest[3 * pps]
    qblk_ref, s_ref, m_ref, sn_ref, carry_ref, acc_ref, l_ref = rest[3 * pps + 1:]
    ph = pl.program_id(1)
    j = pl.program_id(2)

    @pl.when((ph == 0) & (j == 0))
    def _():
        qblk_ref[...] = _head_query_block(q_ref[0].astype(F32), _HPAD, FOX_HD)
        carry_ref[...] = jnp.zeros_like(carry_ref)
        m_ref[...] = jnp.full_like(m_ref, NEG_INF)

    @pl.when(ph == 0)
    def _():
        r_ = lax.broadcasted_iota(jnp.int32, (page, page), 0)
        c_ = lax.broadcasted_iota(jnp.int32, (page, page), 1)
        tri = (c_ > r_).astype(BF16)
        qblk = qblk_ref[...]
        carry = carry_ref[...]
        m = m_ref[...]
        for i in range(pps):
            lf = lf_refs[i][0]
            hi, mid, lo = _split3(lf)
            r = _dot(tri, hi) + _dot(tri, mid) + _dot(tri, lo) + carry
            carry = carry + jnp.sum(lf, axis=0, keepdims=True)
            s = _dot(k_refs[i][0], qblk) + cn_ref[0] + r
            base = pl.multiple_of((j * pps + i) * page, page)
            s_ref[pl.ds(base, page), :] = s
            m = jnp.maximum(m, jnp.max(s, axis=0, keepdims=True))
        carry_ref[...] = carry
        m_ref[...] = m

    @pl.when((ph == 0) & (j == n_steps - 1))
    def _():
        sn = _dot(jnp.broadcast_to(kn_ref[0], (8, FOX_W)), qblk_ref[...])[0:1]
        sn_ref[...] = sn
        m_ref[...] = jnp.maximum(m_ref[...], sn)

    @pl.when((ph == 1) & (j == 0))
    def _():
        acc_ref[...] = jnp.zeros_like(acc_ref)
        l_ref[...] = jnp.zeros_like(l_ref)

    @pl.when(ph == 1)
    def _():
        m = m_ref[...]
        expand = _head_expand_matrix(_HPAD, FOX_HEADS, FOX_HD)
        acc = acc_ref[...]
        l = l_ref[...]
        for i in range(pps):
            base = pl.multiple_of((j * pps + i) * page, page)
            p = jnp.exp(s_ref[pl.ds(base, page), :] - m)
            l = l + jnp.sum(p.reshape(page // 8, 8, _HPAD), axis=0)
            pv = _dot(p, expand) * v_refs[i][0]
            acc = acc + jnp.sum(pv.reshape(page // 8, 8, FOX_W), axis=0)
        acc_ref[...] = acc
        l_ref[...] = l

    @pl.when((ph == 1) & (j == n_steps - 1))
    def _():
        pn = jnp.exp(sn_ref[...] - m_ref[...])
        l = jnp.sum(l_ref[...], axis=0, keepdims=True) + pn
        pne = _dot(jnp.broadcast_to(pn, (8, _HPAD)), _head_expand_matrix(_HPAD, FOX_HEADS, FOX_HD))[0:1]
        acc = jnp.sum(acc_ref[...], axis=0, keepdims=True) + pne * vn_ref[0]
        o_ref[0] = acc / _expand_heads(l, FOX_HEADS, FOX_HD)


def _fox_sample(page_table, fqb, logf, fk, fv, cache_k, cache_v, cache_lf, pps):
    t, n_pages = page_table.shape
    n_pool, page = cache_k.shape[0], cache_k.shape[1]
    n_steps = n_pages // pps
    ck = cache_k.reshape(n_pool, page, FOX_W)
    cv = cache_v.reshape(n_pool, page, FOX_W)

    def k_map(i):
        def f(b, ph, j, pt):
            jj = jnp.where(ph == 0, j, n_steps - 1)
            return (pt[b, n_pages - 1 - (jj * pps + i)], 0, 0)
        return f

    def v_map(i):
        def f(b, ph, j, pt):
            bb = jnp.where(ph == 0, jnp.maximum(b - 1, 0), b)
            jj = jnp.where(ph == 0, n_steps - 1, j)
            return (pt[bb, n_pages - 1 - (jj * pps + i)], 0, 0)
        return f

    tok = lambda w: pl.BlockSpec((1, 1, w), lambda b, ph, j, pt: (b, 0, 0))
    in_specs = [tok(FOX_W), tok(FOX_HEADS), tok(FOX_W), tok(FOX_W)]
    in_specs += [pl.BlockSpec((1, page, FOX_W), k_map(i)) for i in range(pps)]
    in_specs += [pl.BlockSpec((1, page, FOX_HEADS), k_map(i)) for i in range(pps)]
    in_specs += [pl.BlockSpec((1, page, FOX_W), v_map(i)) for i in range(pps)]
    grid_spec = pltpu.PrefetchScalarGridSpec(
        num_scalar_prefetch=1,
        grid=(t, 2, n_steps),
        in_specs=in_specs,
        out_specs=tok(FOX_W),
        scratch_shapes=[pltpu.VMEM((FOX_W, _HPAD), F32), pltpu.VMEM((n_pages * page, _HPAD), F32),
                        pltpu.VMEM((1, _HPAD), F32), pltpu.VMEM((1, _HPAD), F32), pltpu.VMEM((1, _HPAD), F32),
                        pltpu.VMEM((8, FOX_W), F32), pltpu.VMEM((8, _HPAD), F32)],
    )
    o = pl.pallas_call(
        functools.partial(_fox_sample_kernel, pps=pps, page=page, n_steps=n_steps),
        grid_spec=grid_spec,
        out_shape=jax.ShapeDtypeStruct((t, 1, FOX_W), F32),
        compiler_params=_params(("arbitrary", "arbitrary", "arbitrary")),
        name="fox_sample",
    )(page_table, fqb.reshape(t, 1, FOX_W), logf.reshape(t, 1, FOX_HEADS), fk.reshape(t, 1, FOX_W),
      fv.reshape(t, 1, FOX_W), *([ck] * pps), *([cache_lf] * pps), *([cv] * pps))
    return o.reshape(t, FOX_W)


def _memkv_kernel(x_ref, g_ref, w_ref, mk_ref, mv_ref):
    hn = _rms(x_ref[...], g_ref[...]).astype(BF16)
    mk_ref[...] = _dot(hn, w_ref[:, :XA_W])
    mv_ref[...] = _dot(hn, w_ref[:, XA_W:])


def _memkv(mem, g_mem, w_kv, tm):
    rows = mem.shape[0]
    return pl.pallas_call(
        _memkv_kernel,
        grid=(rows // tm,),
        in_specs=[pl.BlockSpec((tm, D_MODEL), lambda i: (i, 0)),
                  pl.BlockSpec((1, D_MODEL), lambda i: (0, 0)),
                  pl.BlockSpec((D_MODEL, 2 * XA_W), lambda i: (0, 0))],
        out_specs=[pl.BlockSpec((tm, XA_W), lambda i: (i, 0))] * 2,
        out_shape=[jax.ShapeDtypeStruct((rows, XA_W), F32)] * 2,
        compiler_params=_params(("arbitrary",)),
        name="memkv",
    )(mem, g_mem.reshape(1, D_MODEL), w_kv)


_FF_CHUNK = 256


def _post_kernel(x_ref, gates_ref, of_ref, or_ref, ox_ref, wf_ref, wr_ref, wx_ref, wo_ref, gffn_ref,
                 wgu_ref, wd_ref, gfin_ref, y_ref):
    mix = None
    for idx, (o_ref, w_ref) in enumerate(((of_ref, wf_ref), (or_ref, wr_ref), (ox_ref, wx_ref))):
        gate = _sigmoid(gates_ref[:, idx * D_MODEL:(idx + 1) * D_MODEL])
        term = gate * _dot(o_ref[...].astype(BF16), w_ref[...])
        mix = term if mix is None else mix + term
    x = x_ref[...] + _dot(mix.astype(BF16), wo_ref[...])
    hb = _rms(x, gffn_ref[...]).astype(BF16)
    ffn = jnp.zeros_like(x)
    for c in range(0, D_FF, _FF_CHUNK):
        u_gate = _dot(hb, wgu_ref[:, c:c + _FF_CHUNK])
        u_up = _dot(hb, wgu_ref[:, D_FF + c:D_FF + c + _FF_CHUNK])
        act = (u_gate * _sigmoid(u_gate) * u_up).astype(BF16)
        ffn = ffn + _dot(act, wd_ref[c:c + _FF_CHUNK, :])
    y_ref[...] = _rms(x + ffn, gfin_ref[...])


def _post(x, gates, o_fox, o_ret, o_xa, wf, wr, wx, wo, g_ffn, wgu, wd, g_final, tm):
    rows = x.shape[0]
    row = lambda w: pl.BlockSpec((tm, w), lambda i: (i, 0))
    const = lambda a, b: pl.BlockSpec((a, b), lambda i: (0, 0))
    return pl.pallas_call(
        _post_kernel,
        grid=(rows // tm,),
        in_specs=[row(D_MODEL), row(3 * D_MODEL), row(FOX_W), row(RET_V_W), row(XA_W),
                  const(FOX_W, D_MODEL), const(RET_V_W, D_MODEL), const(XA_W, D_MODEL),
                  const(D_MODEL, D_MODEL), const(1, D_MODEL), const(D_MODEL, 2 * D_FF),
                  const(D_FF, D_MODEL), const(1, D_MODEL)],
        out_specs=row(D_MODEL),
        out_shape=jax.ShapeDtypeStruct((rows, D_MODEL), F32),
        compiler_params=_params(("arbitrary",)),
        name="post",
    )(x, gates, o_fox, o_ret, o_xa, wf, wr, wx, wo, g_ffn.reshape(1, D_MODEL), wgu, wd,
      g_final.reshape(1, D_MODEL))


def _rotary_tables(pos):
    half = RET_DK // 2
    inv = ROPE_BASE ** (-jnp.arange(half, dtype=F32) / half)
    ang = pos.astype(F32)[:, None] * inv[None, :]
    cos, sin = jnp.cos(ang), jnp.sin(ang)
    zero = jnp.zeros_like(sin)
    tile = lambda a, b: jnp.tile(jnp.concatenate([a, b], axis=1), (1, RET_HEADS))
    return tile(cos, cos), tile(-sin, zero), tile(zero, sin)


def _pick_tile(n, pref):
    t = min(n, pref)
    while n % t:
        t //= 2
    return t


def kernel(x_prompt, x_sample, mem_prompt, cache_fox_k, cache_fox_v, cache_fox_logf, state_ret, cache_mem_k, cache_mem_v, page_table, g_attn, w_in, b_f, g_ret, w_br_fox, w_br_ret, w_br_xa, w_o, g_ffn, w_gu, w_down, g_mem, w_mem_kv, g_final):
    bp, sp, _ = x_prompt.shape
    bs, ts, _ = x_sample.shape
    depth = w_in.shape[0]
    assert depth == 1 and ts == 1
    n_pages, page = page_table.shape[1], cache_fox_k.shape[2]
    past = n_pages * page
    n_mem = mem_prompt.shape[1]
    l = 0

    flog0 = 3 * FOX_W
    w_main = jnp.concatenate([w_in[l][:, :flog0], w_in[l][:, flog0 + FOX_HEADS:]], axis=1).astype(BF16)
    w_fl = w_in[l][:, flog0:flog0 + FOX_HEADS].astype(BF16)
    w_flt = w_fl.T
    wf, wr, wx = w_br_fox[l].astype(BF16), w_br_ret[l].astype(BF16), w_br_xa[l].astype(BF16)
    wo, wgu, wd = w_o[l].astype(BF16), w_gu[l].astype(BF16), w_down[l].astype(BF16)
    w_kv = w_mem_kv[l].astype(BF16)

    mk_p, mv_p = _memkv(mem_prompt.reshape(bp * n_mem, D_MODEL), g_mem[l], w_kv, _pick_tile(bp * n_mem, 256))
    mk_p = mk_p.reshape(bp, n_mem, XA_W)
    mv_p = mv_p.reshape(bp, n_mem, XA_W)
    tm = _pick_tile(sp, 256)
    cos, s1, s2 = _rotary_tables(jnp.arange(sp))
    (gates, fqb, fk, fv, fkb, fvb, logf, ct, rq, rk, rv, rg, xq) = _inproj(
        x_prompt, g_attn[l], w_main, w_fl, w_flt, b_f[l], cos, s1, s2, tm)
    o_fox = _fox_prompt(fqb, fkb, fvb, ct, _pick_tile(sp, 256))
    o_ret, st_p = _ret_prompt(rq, rk, rv, rg, g_ret[l])
    o_xa = _xattn_prompt(xq, mk_p, mv_p, _pick_tile(sp, 256))
    rows = bp * sp
    y_p = _post(x_prompt.reshape(rows, D_MODEL), gates.reshape(rows, 3 * D_MODEL), o_fox.reshape(rows, FOX_W),
                o_ret.reshape(rows, RET_V_W), o_xa.reshape(rows, XA_W), wf, wr, wx, wo, g_ffn[l], wgu, wd,
                g_final, _pick_tile(rows, 256)).reshape(bp, sp, D_MODEL)

    cos_s, s1_s, s2_s = _rotary_tables(jnp.full((bs,), past, jnp.int32))
    (gates_s, fqb_s, fk_s, fv_s, _, _, logf_s, _, rq_s, rk_s, rv_s, rg_s, xq_s) = _inproj(
        x_sample.reshape(1, bs, D_MODEL), g_attn[l], w_main, w_fl, w_flt, b_f[l], cos_s, s1_s, s2_s, bs)
    two = lambda a: a.reshape(bs, a.shape[-1])
    o_fox_s = _fox_sample(page_table, two(fqb_s), two(logf_s), two(fk_s), two(fv_s), cache_fox_k[l],
                          cache_fox_v[l], cache_fox_logf[l], _pick_tile(n_pages, 8))
    o_ret_s, st_s = _ret_sample(two(rq_s), two(rk_s), two(rv_s), two(rg_s), g_ret[l], state_ret[l],
                                _pick_tile(bs, 32))
    o_xa_s = _xattn_sample(two(xq_s), cache_mem_k[l].reshape(bs, n_mem, XA_W),
                           cache_mem_v[l].reshape(bs, n_mem, XA_W))
    y_s = _post(x_sample.reshape(bs, D_MODEL), two(gates_s), o_fox_s, o_ret_s, o_xa_s, wf, wr, wx, wo,
                g_ffn[l], wgu, wd, g_final, bs).reshape(bs, 1, D_MODEL)

    stack = lambda a, shape: a.reshape((1,) + shape)
    return (y_p, y_s,
            stack(fk, (bp, sp, FOX_HEADS, FOX_HD)), stack(fv, (bp, sp, FOX_HEADS, FOX_HD)),
            stack(logf, (bp, sp, FOX_HEADS)), stack(st_p, (bp, RET_HEADS, RET_DK, RET_DV)),
            stack(mk_p, (bp, n_mem, XA_HEADS, XA_HD)), stack(mv_p, (bp, n_mem, XA_HEADS, XA_HD)),
            stack(fk_s, (bs, 1, FOX_HEADS, FOX_HD)), stack(fv_s, (bs, 1, FOX_HEADS, FOX_HD)),
            stack(logf_s, (bs, 1, FOX_HEADS)), stack(st_s, (bs, RET_HEADS, RET_DK, RET_DV)))
```

```python
import functools

import jax
import jax.numpy as jnp
from jax import lax
from jax.experimental import pallas as pl
from jax.experimental.pallas import tpu as pltpu

D_MODEL = 1024
FOX_HEADS = 8
FOX_HD = 64
RET_HEADS = 4
RET_DK = 64
RET_DV = 128
XA_HEADS = 4
XA_HD = 128
FOX_W = FOX_HEADS * FOX_HD
RET_QK_W = RET_HEADS * RET_DK
RET_V_W = RET_HEADS * RET_DV
XA_W = XA_HEADS * XA_HD
D_FF = 2816
RET_CHUNK = 128
ROPE_BASE = 10000.0
EPS = 1e-6
NEG_INF = -1e30

LANES = 128
VMEM_LIMIT = 56 * 1024 * 1024

F32 = jnp.float32
BF16 = jnp.bfloat16

_C_FQ, _C_FK, _C_FV = 0, 512, 1024
_C_RQK, _C_RV, _C_RG, _C_XQ, _C_GATES = 1536, 2048, 2560, 3072, 3584
_W_MAIN = 6656


def _params(sem):
    return pltpu.CompilerParams(dimension_semantics=sem, vmem_limit_bytes=VMEM_LIMIT)


def _dot(a, b):
    return jnp.dot(a, b, preferred_element_type=F32)


def _dot_nt(a, b):
    return lax.dot_general(a, b, (((1,), (1,)), ((), ())), preferred_element_type=F32)


def _dot_tn(a, b):
    return lax.dot_general(a, b, (((0,), (0,)), ((), ())), preferred_element_type=F32)


def _rms(x, g):
    return x * lax.rsqrt(jnp.mean(x * x, axis=-1, keepdims=True) + EPS) * g


def _log_sigmoid(x):
    return -(jnp.maximum(-x, 0.0) + jnp.log1p(jnp.exp(-jnp.abs(x))))


def _sigmoid(x):
    return 1.0 / (1.0 + jnp.exp(-x))


def _split3(x):
    hi = x.astype(BF16)
    r1 = x - hi.astype(F32)
    mid = r1.astype(BF16)
    lo = (r1 - mid.astype(F32)).astype(BF16)
    return hi, mid, lo


def _expand_heads(x, n_heads, head_w):
    width = n_heads * head_w
    lane = lax.broadcasted_iota(jnp.int32, (1, width), 1)
    out = jnp.zeros((1, width), F32)
    for h in range(n_heads):
        sel = (lane >= h * head_w) & (lane < (h + 1) * head_w)
        out = jnp.where(sel, x[:, h:h + 1], out)
    return out


def _inproj_kernel(x_ref, g_ref, wm_ref, wkvt_ref, wflt_ref, bft_ref, cos_ref, s1_ref, s2_ref,
                   gates_ref, fq_ref, fkt_ref, fvt_ref, fktb_ref, fkb_ref, fvb_ref, lft_ref, ct_ref,
                   rq_ref, rk_ref, rv_ref, rg_ref, xq_ref, carry_ref, *, tm):
    j = pl.program_id(1)
    hn = _rms(x_ref[0], g_ref[...]).astype(BF16)

    def proj(c0, width):
        return _dot(hn, wm_ref[:, c0:c0 + width])

    fq_ref[0] = (proj(_C_FQ, FOX_W) * (FOX_HD ** -0.5)).astype(BF16)
    fkb_ref[0] = proj(_C_FK, FOX_W).astype(BF16)
    fvb_ref[0] = proj(_C_FV, FOX_W).astype(BF16)
    fkt = _dot_nt(wkvt_ref[:FOX_W, :], hn)
    fkt_ref[0] = fkt
    fktb_ref[0] = fkt.astype(BF16)
    fvt_ref[0] = _dot_nt(wkvt_ref[FOX_W:, :], hn)

    rqk = proj(_C_RQK, 2 * RET_QK_W)
    cos, s1, s2 = cos_ref[...], s1_ref[...], s2_ref[...]
    for ref, off in ((rq_ref, 0), (rk_ref, RET_QK_W)):
        r = rqk[:, off:off + RET_QK_W]
        ref[0] = (r * cos + pltpu.roll(r, RET_QK_W - RET_DK // 2, 1) * s1
                  + pltpu.roll(r, RET_DK // 2, 1) * s2)
    rv_ref[0] = proj(_C_RV, RET_V_W)
    rg_ref[0] = proj(_C_RG, RET_V_W)
    xq_ref[0] = proj(_C_XQ, XA_W)
    for c in range(0, 3 * D_MODEL, 512):
        gates_ref[0, :, c:c + 512] = proj(_C_GATES + c, 512)

    lft = _log_sigmoid(_dot_nt(wflt_ref[...], hn) + bft_ref[...])
    lft_ref[0] = lft

    @pl.when(j == 0)
    def _():
        carry_ref[...] = jnp.zeros_like(carry_ref)

    row = lax.broadcasted_iota(jnp.int32, (tm, tm), 0)
    col = lax.broadcasted_iota(jnp.int32, (tm, tm), 1)
    tri = (row <= col).astype(BF16)
    hi, mid, lo = _split3(lft)
    ct = _dot(hi, tri) + _dot(mid, tri) + _dot(lo, tri) + carry_ref[:, 0:1]
    ct_ref[0] = ct
    carry_ref[...] = jnp.broadcast_to(ct[:, tm - 1:tm], carry_ref.shape)


def _inproj(x, g_attn, w_main, w_kvt, w_flt, b_f, cos, s1, s2, tm):
    b, s, _ = x.shape
    grid = (b, s // tm)
    tok = lambda w: pl.BlockSpec((1, tm, w), lambda i, j: (i, j, 0))
    feat = lambda h: pl.BlockSpec((1, h, tm), lambda i, j: (i, 0, j))
    const = lambda shape: pl.BlockSpec(shape, lambda i, j: (0,) * len(shape))
    tab = pl.BlockSpec((tm, RET_QK_W), lambda i, j: (j, 0))
    outs = [
        (tok, 3 * D_MODEL, F32), (tok, FOX_W, BF16),
        (feat, FOX_W, F32), (feat, FOX_W, F32), (feat, FOX_W, BF16),
        (tok, FOX_W, BF16), (tok, FOX_W, BF16),
        (feat, FOX_HEADS, F32), (feat, FOX_HEADS, F32),
        (tok, RET_QK_W, F32), (tok, RET_QK_W, F32), (tok, RET_V_W, F32), (tok, RET_V_W, F32), (tok, XA_W, F32),
    ]
    shape = lambda kind, w: (b, s, w) if kind is tok else (b, w, s)
    return pl.pallas_call(
        functools.partial(_inproj_kernel, tm=tm),
        grid=grid,
        in_specs=[tok(D_MODEL), const((1, D_MODEL)), const((D_MODEL, _W_MAIN)), const((2 * FOX_W, D_MODEL)),
                  const((FOX_HEADS, D_MODEL)), const((FOX_HEADS, 1)), tab, tab, tab],
        out_specs=[kind(w) for kind, w, _ in outs],
        out_shape=[jax.ShapeDtypeStruct(shape(kind, w), dt) for kind, w, dt in outs],
        scratch_shapes=[pltpu.VMEM((FOX_HEADS, LANES), F32)],
        compiler_params=_params(("arbitrary", "arbitrary")),
        name="inproj",
    )(x, g_attn.reshape(1, D_MODEL), w_main, w_kvt, w_flt, b_f.reshape(FOX_HEADS, 1), cos, s1, s2)


def _lane_blocks(x, op):
    out = x[:, :LANES]
    for c in range(LANES, x.shape[1], LANES):
        out = op(out, x[:, c:c + LANES])
    return out


_LOG2E = 1.4426950408889634


def _fox_prompt_kernel(q_ref, kt_ref, v_ref, ct_ref, o_ref, s_ref, mx_ref, l_ref, acc_ref, *, t):
    i = pl.program_id(2)
    lane = lax.broadcasted_iota(jnp.int32, (t, LANES), 1)
    row = lax.broadcasted_iota(jnp.int32, (t, t), 0)
    col = lax.broadcasted_iota(jnp.int32, (t, t), 1)
    chains = [(hh, half) for hh in range(2) for half in range(2)]
    qm = []
    for hh, half in chains:
        q = q_ref[0, half * t:(half + 1) * t, :]
        head_lanes = (lane >= hh * FOX_HD) & (lane < (hh + 1) * FOX_HD)
        qm.append(jnp.where(head_lanes, q, jnp.zeros_like(q)))

    def score_tile(c, j, diag):
        hh = chains[c][0]
        off = pl.multiple_of(j * t, t)
        s = (_dot(qm[c], kt_ref[0, :, pl.ds(off, t)]) - ct_ref[0, 0, hh:hh + 1, pl.ds(off, t)]) * _LOG2E
        if diag:
            s = jnp.where(col <= row, s, NEG_INF)
        s_ref[c, j] = s
        mx_ref[c] = jnp.maximum(mx_ref[c], _lane_blocks(s, jnp.maximum))

    tail = {0: ((0, True),), 1: ((0, False), (1, True))}

    mx_ref[...] = jnp.full_like(mx_ref, NEG_INF)

    def pass1(j, carry):
        for c in range(4):
            score_tile(c, j, False)
        return carry

    lax.fori_loop(0, 2 * i, pass1, 0)
    for c, (_, half) in enumerate(chains):
        for dj, diag in tail[half]:
            score_tile(c, 2 * i + dj, diag)

    m = [jnp.max(mx_ref[c], axis=1, keepdims=True) for c in range(4)]
    l_ref[...] = jnp.zeros_like(l_ref)
    acc_ref[...] = jnp.zeros_like(acc_ref)

    def value_tile(c, j):
        off = pl.multiple_of(j * t, t)
        p = jnp.exp2(s_ref[c, j] - m[c])
        l_ref[c] += _lane_blocks(p, jnp.add)
        acc_ref[c] += _dot(p.astype(BF16), v_ref[0, pl.ds(off, t), :])

    def pass2(j, carry):
        for c in range(4):
            value_tile(c, j)
        return carry

    lax.fori_loop(0, 2 * i, pass2, 0)
    for c, (_, half) in enumerate(chains):
        for dj, _ in tail[half]:
            value_tile(c, 2 * i + dj)

    out = [acc_ref[c] / jnp.sum(l_ref[c], axis=1, keepdims=True) for c in range(4)]
    for half in range(2):
        o_ref[0, half * t:(half + 1) * t, :] = jnp.where(lane < FOX_HD, out[half], out[2 + half])


def _fox_prompt(fqb, fktb, fvb, ct, t):
    b, s, _ = fqb.shape
    pairs = FOX_HEADS // 2
    ct4 = ct.reshape(b, pairs, 2, s)
    return pl.pallas_call(
        functools.partial(_fox_prompt_kernel, t=t),
        grid=(b, pairs, s // (2 * t)),
        in_specs=[pl.BlockSpec((1, 2 * t, LANES), lambda bi, pr, i: (bi, i, pr)),
                  pl.BlockSpec((1, LANES, s), lambda bi, pr, i: (bi, pr, 0)),
                  pl.BlockSpec((1, s, LANES), lambda bi, pr, i: (bi, 0, pr)),
                  pl.BlockSpec((1, 1, 2, s), lambda bi, pr, i: (bi, pr, 0, 0))],
        out_specs=pl.BlockSpec((1, 2 * t, LANES), lambda bi, pr, i: (bi, i, pr)),
        out_shape=jax.ShapeDtypeStruct((b, s, FOX_W), F32),
        scratch_shapes=[pltpu.VMEM((4, s // t, t, t), F32), pltpu.VMEM((4, t, LANES), F32),
                        pltpu.VMEM((4, t, LANES), F32), pltpu.VMEM((4, t, LANES), F32)],
        compiler_params=_params(("arbitrary", "arbitrary", "arbitrary")),
        name="fox_prompt",
    )(fqb, fktb, fvb, ct4)


def _ret_prompt_kernel(q_ref, k_ref, v_ref, rg_ref, gr_ref, dmask_ref, qdec_ref, kdec_ref, cdec_ref,
                       o_ref, st_ref):
    n = pl.program_id(2)

    @pl.when(n == 0)
    def _():
        st_ref[...] = jnp.zeros_like(st_ref)

    c = RET_CHUNK
    lane = lax.broadcasted_iota(jnp.int32, (c, LANES), 1)
    q = q_ref[0]
    k = k_ref[0] * (RET_DK ** -0.5)
    state = st_ref[0, 0]
    qd = (q * qdec_ref[0]).astype(BF16)
    kd = k * kdec_ref[0]
    kb = k.astype(BF16)
    new_state = cdec_ref[0] * state
    state_b = state.astype(BF16)
    for hh in range(2):
        head_lanes = (lane >= hh * RET_DK) & (lane < (hh + 1) * RET_DK)
        v = v_ref[0, :, hh * RET_DV:(hh + 1) * RET_DV]
        vb = v.astype(BF16)
        qm = jnp.where(head_lanes, q, 0.0).astype(BF16)
        inner = _dot_nt(qm, kb) * dmask_ref[0, hh]
        qdm = jnp.where(head_lanes, qd, jnp.zeros_like(qd))
        o = _dot(inner.astype(BF16), vb) + _dot(qdm, state_b)
        kdm = jnp.where(head_lanes, kd, 0.0).astype(BF16)
        new_state = new_state + _dot_tn(kdm, vb)
        mu = jnp.mean(o, axis=-1, keepdims=True)
        var = jnp.mean(jnp.square(o - mu), axis=-1, keepdims=True)
        y = (o - mu) * lax.rsqrt(var + EPS) * gr_ref[:, hh * RET_DV:(hh + 1) * RET_DV]
        rg = rg_ref[0, :, hh * RET_DV:(hh + 1) * RET_DV]
        o_ref[0, :, hh * RET_DV:(hh + 1) * RET_DV] = rg * _sigmoid(rg) * y
    st_ref[0, 0] = new_state


def _ret_tables(length):
    h = RET_HEADS
    log_g = jnp.log(1.0 - 2.0 ** (-5.0 - jnp.arange(h, dtype=F32)))
    i = jnp.arange(length, dtype=F32)
    diff = i[:, None] - i[None, :]
    dmask = jnp.where(diff[None] >= 0, jnp.exp(jnp.maximum(diff, 0.0)[None] * log_g[:, None, None]), 0.0)
    q_dec = jnp.exp((i + 1.0)[:, None] * log_g[None, :])
    k_dec = jnp.exp((length - 1.0 - i)[:, None] * log_g[None, :])
    chunk_dec = jnp.exp(length * log_g)
    return dmask, q_dec, k_dec, chunk_dec


def _ret_prompt(rq, rk, rv, rg, g_ret):
    b, s, _ = rq.shape
    c = RET_CHUNK
    pairs = RET_HEADS // 2
    dmask, q_dec, k_dec, chunk_dec = _ret_tables(c)
    dmask = dmask.reshape(pairs, 2, c, c)
    qdec = jnp.repeat(q_dec, RET_DK, axis=1).reshape(c, pairs, LANES).transpose(1, 0, 2)
    kdec = jnp.repeat(k_dec, RET_DK, axis=1).reshape(c, pairs, LANES).transpose(1, 0, 2)
    cdec = jnp.repeat(chunk_dec, RET_DK).reshape(pairs, LANES, 1)
    o, st = pl.pallas_call(
        _ret_prompt_kernel,
        grid=(b, pairs, s // c),
        in_specs=[pl.BlockSpec((1, c, LANES), lambda bi, pr, n: (bi, n, pr)),
                  pl.BlockSpec((1, c, LANES), lambda bi, pr, n: (bi, n, pr)),
                  pl.BlockSpec((1, c, 2 * RET_DV), lambda bi, pr, n: (bi, n, pr)),
                  pl.BlockSpec((1, c, 2 * RET_DV), lambda bi, pr, n: (bi, n, pr)),
                  pl.BlockSpec((1, 2 * RET_DV), lambda bi, pr, n: (0, pr)),
                  pl.BlockSpec((1, 2, c, c), lambda bi, pr, n: (pr, 0, 0, 0)),
                  pl.BlockSpec((1, c, LANES), lambda bi, pr, n: (pr, 0, 0)),
                  pl.BlockSpec((1, c, LANES), lambda bi, pr, n: (pr, 0, 0)),
                  pl.BlockSpec((1, LANES, 1), lambda bi, pr, n: (pr, 0, 0))],
        out_specs=[pl.BlockSpec((1, c, 2 * RET_DV), lambda bi, pr, n: (bi, n, pr)),
                   pl.BlockSpec((1, 1, 2 * RET_DK, RET_DV), lambda bi, pr, n: (bi, pr, 0, 0))],
        out_shape=[jax.ShapeDtypeStruct((b, s, RET_V_W), F32),
                   jax.ShapeDtypeStruct((b, pairs, 2 * RET_DK, RET_DV), F32)],
        compiler_params=_params(("arbitrary", "arbitrary", "arbitrary")),
        name="ret_prompt",
    )(rq, rk, rv, rg, g_ret.reshape(1, RET_V_W), dmask, qdec, kdec, cdec)
    return o, st.reshape(b, RET_HEADS, RET_DK, RET_DV)


def _ret_sample_kernel(q_ref, k_ref, v_ref, rg_ref, gr_ref, qdec_ref, cdec_ref, s0_ref, o_ref, s1_ref, *, bt):
    rows = RET_HEADS * RET_DK
    q = q_ref[...]
    k = k_ref[...] * (RET_DK ** -0.5)
    qg = q * qdec_ref[...]
    qk = q * k
    for h in range(RET_HEADS):
        v = v_ref[:, h * RET_DV:(h + 1) * RET_DV]
        cd = cdec_ref[:, h * RET_DV:(h + 1) * RET_DV]
        inner = jnp.sum(qk[:, h * RET_DK:(h + 1) * RET_DK], axis=1, keepdims=True)
        o = inner * v
        for d in range(RET_DK):
            r = h * RET_DK + d
            srow = s0_ref[pl.ds(r, bt, stride=rows), :]
            o = o + qg[:, r:r + 1] * srow
            s1_ref[pl.ds(r, bt, stride=rows), :] = cd * srow + k[:, r:r + 1] * v
        mu = jnp.mean(o, axis=-1, keepdims=True)
        var = jnp.mean(jnp.square(o - mu), axis=-1, keepdims=True)
        y = (o - mu) * lax.rsqrt(var + EPS) * gr_ref[:, h * RET_DV:(h + 1) * RET_DV]
        rg = rg_ref[:, h * RET_DV:(h + 1) * RET_DV]
        o_ref[:, h * RET_DV:(h + 1) * RET_DV] = rg * _sigmoid(rg) * y


def _ret_sample(rq, rk, rv, rg, g_ret, s0, bt):
    t = rq.shape[0]
    rows = RET_HEADS * RET_DK
    _, q_dec, _, chunk_dec = _ret_tables(1)
    qdec = jnp.repeat(q_dec, RET_DK, axis=1).reshape(1, RET_QK_W)
    cdec = jnp.repeat(chunk_dec, RET_DV).reshape(1, RET_V_W)
    tok = lambda w: pl.BlockSpec((bt, w), lambda i: (i, 0))
    const = lambda w: pl.BlockSpec((1, w), lambda i: (0, 0))
    st = pl.BlockSpec((bt * rows, RET_DV), lambda i: (i, 0))
    o, s1 = pl.pallas_call(
        functools.partial(_ret_sample_kernel, bt=bt),
        grid=(t // bt,),
        in_specs=[tok(RET_QK_W), tok(RET_QK_W), tok(RET_V_W), tok(RET_V_W), const(RET_V_W),
                  const(RET_QK_W), const(RET_V_W), st],
        out_specs=[tok(RET_V_W), st],
        out_shape=[jax.ShapeDtypeStruct((t, RET_V_W), F32), jax.ShapeDtypeStruct((t * rows, RET_DV), F32)],
        compiler_params=_params(("arbitrary",)),
        name="ret_sample",
    )(rq, rk, rv, rg, g_ret.reshape(1, RET_V_W), qdec, cdec, s0.reshape(t * rows, RET_DV))
    return o, s1.reshape(t, RET_HEADS, RET_DK, RET_DV)


def _xattn_prompt_kernel(q_ref, mk_ref, mv_ref, o_ref):
    scale = XA_HD ** -0.5
    for h in range(XA_HEADS):
        sl = slice(h * XA_HD, (h + 1) * XA_HD)
        s = _dot_nt(q_ref[0, :, sl].astype(BF16), mk_ref[0, :, sl].astype(BF16)) * scale
        p = jnp.exp(s - jnp.max(s, axis=1, keepdims=True))
        p = p / jnp.sum(p, axis=1, keepdims=True)
        o_ref[0, :, sl] = _dot(p.astype(BF16), mv_ref[0, :, sl].astype(BF16))


def _xattn_prompt(xq, mk, mv, t):
    b, s, _ = xq.shape
    m = mk.shape[1]
    return pl.pallas_call(
        _xattn_prompt_kernel,
        grid=(b, s // t),
        in_specs=[pl.BlockSpec((1, t, XA_W), lambda bi, i: (bi, i, 0)),
                  pl.BlockSpec((1, m, XA_W), lambda bi, i: (bi, 0, 0)),
                  pl.BlockSpec((1, m, XA_W), lambda bi, i: (bi, 0, 0))],
        out_specs=pl.BlockSpec((1, t, XA_W), lambda bi, i: (bi, i, 0)),
        out_shape=jax.ShapeDtypeStruct((b, s, XA_W), F32),
        compiler_params=_params(("arbitrary", "arbitrary")),
        name="xattn_prompt",
    )(xq, mk, mv)


def _head_query_block(q_row, n_cols, head_w):
    w = q_row.shape[1]
    r = lax.broadcasted_iota(jnp.int32, (w, w), 0)
    c = lax.broadcasted_iota(jnp.int32, (w, w), 1)
    diag = jnp.where(r == c, jnp.broadcast_to(q_row, (w, w)), 0.0)
    rr = lax.broadcasted_iota(jnp.int32, (w, n_cols), 0)
    cc = lax.broadcasted_iota(jnp.int32, (w, n_cols), 1)
    ones = ((rr >= cc * head_w) & (rr < (cc + 1) * head_w)).astype(F32)
    return _dot(diag, ones)


def _head_expand_matrix(n_rows, n_heads, head_w):
    width = n_heads * head_w
    rr = lax.broadcasted_iota(jnp.int32, (n_rows, width), 0)
    cc = lax.broadcasted_iota(jnp.int32, (n_rows, width), 1)
    return ((cc >= rr * head_w) & (cc < (rr + 1) * head_w)).astype(F32)


_HPAD = 8


def _xattn_sample_kernel(q_ref, mk_ref, mv_ref, o_ref):
    qblk = _head_query_block(q_ref[0], _HPAD, XA_HD)
    s = _dot(mk_ref[0], qblk) * (XA_HD ** -0.5)
    m = jnp.max(s, axis=0, keepdims=True)
    p = jnp.exp(s - m)
    l = jnp.sum(p, axis=0, keepdims=True)
    pe = _dot(p, _head_expand_matrix(_HPAD, XA_HEADS, XA_HD))
    acc = jnp.sum(pe * mv_ref[0], axis=0, keepdims=True)
    o_ref[0] = acc / _expand_heads(l, XA_HEADS, XA_HD)


def _xattn_sample(xq, mk, mv):
    t, m, _ = mk.shape
    o = pl.pallas_call(
        _xattn_sample_kernel,
        grid=(t,),
        in_specs=[pl.BlockSpec((1, 1, XA_W), lambda i: (i, 0, 0)),
                  pl.BlockSpec((1, m, XA_W), lambda i: (i, 0, 0)),
                  pl.BlockSpec((1, m, XA_W), lambda i: (i, 0, 0))],
        out_specs=pl.BlockSpec((1, 1, XA_W), lambda i: (i, 0, 0)),
        out_shape=jax.ShapeDtypeStruct((t, 1, XA_W), F32),
        compiler_params=_params(("arbitrary",)),
        name="xattn_sample",
    )(xq.reshape(t, 1, XA_W), mk, mv)
    return o.reshape(t, XA_W)


def _suffix_sum_lanes(x):
    n = x.shape[1]
    lane = lax.broadcasted_iota(jnp.int32, x.shape, 1)
    sh = 1
    while sh < n:
        x = x + jnp.where(lane < n - sh, pltpu.roll(x, n - sh, 1), 0.0)
        sh *= 2
    return x


def _fox_sample_kernel(pt_ref, q_ref, cn_ref, kn_ref, vn_ref, *rest, pps, n_steps):
    k_refs = rest[:pps]
    lf_refs = rest[pps:2 * pps]
    v_refs = rest[2 * pps:3 * pps]
    o_ref = rest[3 * pps]
    qb_ref, s_ref, m_ref, sn_ref, cnb_ref, carry_ref, acc_ref, l_ref = rest[3 * pps + 1:]
    ph = pl.program_id(1)
    j = pl.program_id(2)
    hrow = lax.broadcasted_iota(jnp.int32, (FOX_HEADS, FOX_W), 0)
    hlane = lax.broadcasted_iota(jnp.int32, (FOX_HEADS, FOX_W), 1)
    head_mask = (hlane >= hrow * FOX_HD) & (hlane < (hrow + 1) * FOX_HD)

    def rows_to_col(row_vals):
        r8 = lax.broadcasted_iota(jnp.int32, (FOX_HEADS, FOX_HEADS), 0)
        c8 = lax.broadcasted_iota(jnp.int32, (FOX_HEADS, FOX_HEADS), 1)
        b = jnp.broadcast_to(row_vals, (FOX_HEADS, FOX_HEADS))
        return jnp.sum(jnp.where(r8 == c8, b, 0.0), axis=1, keepdims=True)

    def col_to_head_row(col_vals):
        b = jnp.broadcast_to(col_vals, (FOX_HEADS, FOX_W))
        return jnp.sum(jnp.where(head_mask, b, 0.0), axis=0, keepdims=True)

    @pl.when((ph == 0) & (j == 0))
    def _():
        q = q_ref[0]
        qb_ref[...] = jnp.transpose(jnp.broadcast_to(q, (LANES, FOX_W)))
        cnb_ref[...] = jnp.broadcast_to(rows_to_col(cn_ref[0]), cnb_ref.shape)
        carry_ref[...] = jnp.zeros_like(carry_ref)
        m_ref[...] = jnp.full_like(m_ref, NEG_INF)

    @pl.when(ph == 0)
    def _():
        qb = qb_ref[...]
        cnb = cnb_ref[...]
        carry = carry_ref[...]
        m = m_ref[...]
        for i in range(pps):
            lf = lf_refs[i][0]
            suf = _suffix_sum_lanes(lf)
            prod = k_refs[i][0].reshape(FOX_W, LANES) * qb
            s = jnp.sum(prod.reshape(FOX_HEADS, FOX_HD, LANES), axis=1) + cnb + ((suf - lf) + carry)
            s_ref[j * pps + i] = s
            m = jnp.maximum(m, s)
            carry = carry + suf[:, 0:1]
        carry_ref[...] = carry
        m_ref[...] = m

    @pl.when((ph == 0) & (j == n_steps - 1))
    def _():
        q8 = jnp.where(head_mask, jnp.broadcast_to(q_ref[0], (FOX_HEADS, FOX_W)), 0.0)
        sn = jnp.sum(q8 * kn_ref[0], axis=1, keepdims=True)
        m = jnp.maximum(jnp.max(m_ref[...], axis=1, keepdims=True), sn)
        sn_ref[...] = jnp.broadcast_to(sn, sn_ref.shape)
        m_ref[...] = jnp.broadcast_to(m, m_ref.shape)

    @pl.when((ph == 1) & (j == 0))
    def _():
        acc_ref[...] = jnp.zeros_like(acc_ref)
        l_ref[...] = jnp.zeros_like(l_ref)

    @pl.when(ph == 1)
    def _():
        m = m_ref[...]
        ps = [jnp.exp(s_ref[j * pps + i] - m) for i in range(pps)]
        l = l_ref[...]
        for p in ps:
            l = l + p
        l_ref[...] = l
        for h in range(FOX_HEADS):
            a = acc_ref[h]
            for i in range(pps):
                a = a + ps[i][h:h + 1, :] * v_refs[i][0, h]
            acc_ref[h] = a

    @pl.when((ph == 1) & (j == n_steps - 1))
    def _():
        pn = jnp.exp(sn_ref[:, 0:1] - m_ref[:, 0:1])
        l = jnp.sum(l_ref[...], axis=1, keepdims=True) + pn
        acc_t = jnp.transpose(acc_ref[...].reshape(FOX_W, LANES))
        acc = jnp.sum(acc_t, axis=0, keepdims=True) + col_to_head_row(pn) * vn_ref[0]
        o_ref[0] = acc / col_to_head_row(l)


def _fox_sample(page_table, fq, logf, fk, fv, ckt, cvt, clft, pps):
    t, n_pages = page_table.shape
    page = ckt.shape[3]
    assert page == LANES
    n_steps = n_pages // pps

    def k_map(i, nd):
        def f(b, ph, j, pt):
            jj = jnp.where(ph == 0, j, n_steps - 1)
            return (pt[b, n_pages - 1 - (jj * pps + i)],) + (0,) * nd
        return f

    def v_map(i):
        def f(b, ph, j, pt):
            bb = jnp.where(ph == 0, jnp.maximum(b - 1, 0), b)
            jj = jnp.where(ph == 0, n_steps - 1, j)
            return (pt[bb, n_pages - 1 - (jj * pps + i)], 0, 0, 0)
        return f

    tok = lambda w: pl.BlockSpec((1, 1, w), lambda b, ph, j, pt: (b, 0, 0))
    hb = (FOX_HEADS, LANES)
    in_specs = [tok(FOX_W), tok(FOX_HEADS), tok(FOX_W), tok(FOX_W)]
    in_specs += [pl.BlockSpec((1, FOX_HEADS, FOX_HD, page), k_map(i, 3)) for i in range(pps)]
    in_specs += [pl.BlockSpec((1, FOX_HEADS, page), k_map(i, 2)) for i in range(pps)]
    in_specs += [pl.BlockSpec((1, FOX_HEADS, FOX_HD, page), v_map(i)) for i in range(pps)]
    grid_spec = pltpu.PrefetchScalarGridSpec(
        num_scalar_prefetch=1,
        grid=(t, 2, n_steps),
        in_specs=in_specs,
        out_specs=tok(FOX_W),
        scratch_shapes=[pltpu.VMEM((FOX_W, LANES), F32), pltpu.VMEM((n_pages,) + hb, F32),
                        pltpu.VMEM(hb, F32), pltpu.VMEM(hb, F32), pltpu.VMEM(hb, F32), pltpu.VMEM(hb, F32),
                        pltpu.VMEM((FOX_HEADS, FOX_HD, LANES), F32), pltpu.VMEM(hb, F32)],
    )
    o = pl.pallas_call(
        functools.partial(_fox_sample_kernel, pps=pps, n_steps=n_steps),
        grid_spec=grid_spec,
        out_shape=jax.ShapeDtypeStruct((t, 1, FOX_W), F32),
        compiler_params=_params(("arbitrary", "arbitrary", "arbitrary")),
        name="fox_sample",
    )(page_table, fq.reshape(t, 1, FOX_W), logf.reshape(t, 1, FOX_HEADS), fk.reshape(t, 1, FOX_W),
      fv.reshape(t, 1, FOX_W), *([ckt] * pps), *([clft] * pps), *([cvt] * pps))
    return o.reshape(t, FOX_W)


def _memkv_kernel(x_ref, g_ref, w_ref, mk_ref, mv_ref):
    hn = _rms(x_ref[...], g_ref[...]).astype(BF16)
    mk_ref[...] = _dot(hn, w_ref[:, :XA_W])
    mv_ref[...] = _dot(hn, w_ref[:, XA_W:])


def _memkv(mem, g_mem, w_kv, tm):
    rows = mem.shape[0]
    return pl.pallas_call(
        _memkv_kernel,
        grid=(rows // tm,),
        in_specs=[pl.BlockSpec((tm, D_MODEL), lambda i: (i, 0)),
                  pl.BlockSpec((1, D_MODEL), lambda i: (0, 0)),
                  pl.BlockSpec((D_MODEL, 2 * XA_W), lambda i: (0, 0))],
        out_specs=[pl.BlockSpec((tm, XA_W), lambda i: (i, 0))] * 2,
        out_shape=[jax.ShapeDtypeStruct((rows, XA_W), F32)] * 2,
        compiler_params=_params(("arbitrary",)),
        name="memkv",
    )(mem, g_mem.reshape(1, D_MODEL), w_kv)


_FF_CHUNK = 256


def _post_kernel(x_ref, gates_ref, of_ref, or_ref, ox_ref, wf_ref, wr_ref, wx_ref, wo_ref, gffn_ref,
                 wgu_ref, wd_ref, gfin_ref, y_ref):
    mix = None
    for idx, (o_ref, w_ref) in enumerate(((of_ref, wf_ref), (or_ref, wr_ref), (ox_ref, wx_ref))):
        gate = _sigmoid(gates_ref[:, idx * D_MODEL:(idx + 1) * D_MODEL])
        term = gate * _dot(o_ref[...].astype(BF16), w_ref[...])
        mix = term if mix is None else mix + term
    x = x_ref[...] + _dot(mix.astype(BF16), wo_ref[...])
    hb = _rms(x, gffn_ref[...]).astype(BF16)
    ffn = jnp.zeros_like(x)
    for c in range(0, D_FF, _FF_CHUNK):
        u_gate = _dot(hb, wgu_ref[:, c:c + _FF_CHUNK])
        u_up = _dot(hb, wgu_ref[:, D_FF + c:D_FF + c + _FF_CHUNK])
        act = (u_gate * _sigmoid(u_gate) * u_up).astype(BF16)
        ffn = ffn + _dot(act, wd_ref[c:c + _FF_CHUNK, :])
    y_ref[...] = _rms(x + ffn, gfin_ref[...])


def _post(x, gates, o_fox, o_ret, o_xa, wf, wr, wx, wo, g_ffn, wgu, wd, g_final, tm):
    rows = x.shape[0]
    row = lambda w: pl.BlockSpec((tm, w), lambda i: (i, 0))
    const = lambda a, b: pl.BlockSpec((a, b), lambda i: (0, 0))
    return pl.pallas_call(
        _post_kernel,
        grid=(rows // tm,),
        in_specs=[row(D_MODEL), row(3 * D_MODEL), row(FOX_W), row(RET_V_W), row(XA_W),
                  const(FOX_W, D_MODEL), const(RET_V_W, D_MODEL), const(XA_W, D_MODEL),
                  const(D_MODEL, D_MODEL), const(1, D_MODEL), const(D_MODEL, 2 * D_FF),
                  const(D_FF, D_MODEL), const(1, D_MODEL)],
        out_specs=row(D_MODEL),
        out_shape=jax.ShapeDtypeStruct((rows, D_MODEL), F32),
        compiler_params=_params(("arbitrary",)),
        name="post",
    )(x, gates, o_fox, o_ret, o_xa, wf, wr, wx, wo, g_ffn.reshape(1, D_MODEL), wgu, wd,
      g_final.reshape(1, D_MODEL))


def _rotary_tables(pos):
    half = RET_DK // 2
    inv = ROPE_BASE ** (-jnp.arange(half, dtype=F32) / half)
    ang = pos.astype(F32)[:, None] * inv[None, :]
    cos, sin = jnp.cos(ang), jnp.sin(ang)
    zero = jnp.zeros_like(sin)
    tile = lambda a, b: jnp.tile(jnp.concatenate([a, b], axis=1), (1, RET_HEADS))
    return tile(cos, cos), tile(-sin, zero), tile(zero, sin)


def _pick_tile(n, pref):
    t = min(n, pref)
    while n % t:
        t //= 2
    return t


def kernel(x_prompt, x_sample, mem_prompt, cache_fox_k, cache_fox_v, cache_fox_logf, state_ret, cache_mem_k, cache_mem_v, page_table, g_attn, w_in, b_f, g_ret, w_br_fox, w_br_ret, w_br_xa, w_o, g_ffn, w_gu, w_down, g_mem, w_mem_kv, g_final):
    bp, sp, _ = x_prompt.shape
    bs, ts, _ = x_sample.shape
    depth = w_in.shape[0]
    assert depth == 1 and ts == 1
    n_pages, page = page_table.shape[1], cache_fox_k.shape[2]
    past = n_pages * page
    n_mem = mem_prompt.shape[1]
    l = 0

    flog0 = 3 * FOX_W
    w_l = w_in[l]
    w_main = jnp.concatenate([w_l[:, :flog0], w_l[:, flog0 + FOX_HEADS:]], axis=1).astype(BF16)
    w_kvt = w_l[:, FOX_W:flog0].T.astype(BF16)
    w_flt = w_l[:, flog0:flog0 + FOX_HEADS].T.astype(BF16)
    wf, wr, wx = w_br_fox[l].astype(BF16), w_br_ret[l].astype(BF16), w_br_xa[l].astype(BF16)
    wo, wgu, wd = w_o[l].astype(BF16), w_gu[l].astype(BF16), w_down[l].astype(BF16)
    w_kv = w_mem_kv[l].astype(BF16)

    def token_major(a_t, n_heads, head_w):
        b, _, s = a_t.shape
        return a_t.reshape(b, n_heads, head_w, s).transpose(0, 3, 1, 2)[None]

    mk_p, mv_p = _memkv(mem_prompt.reshape(bp * n_mem, D_MODEL), g_mem[l], w_kv, _pick_tile(bp * n_mem, 256))
    mk_p = mk_p.reshape(bp, n_mem, XA_W)
    mv_p = mv_p.reshape(bp, n_mem, XA_W)
    tm = _pick_tile(sp, 256)
    cos, s1, s2 = _rotary_tables(jnp.arange(sp))
    (gates, fqb, fkt, fvt, fktb, _, fvb, lft, ct, rq, rk, rv, rg, xq) = _inproj(
        x_prompt, g_attn[l], w_main, w_kvt, w_flt, b_f[l], cos, s1, s2, tm)
    o_fox = _fox_prompt(fqb, fktb, fvb, ct, _pick_tile(sp // 2, 512))
    o_ret, st_p = _ret_prompt(rq, rk, rv, rg, g_ret[l])
    o_xa = _xattn_prompt(xq, mk_p, mv_p, _pick_tile(sp, 256))
    rows = bp * sp
    y_p = _post(x_prompt.reshape(rows, D_MODEL), gates.reshape(rows, 3 * D_MODEL), o_fox.reshape(rows, FOX_W),
                o_ret.reshape(rows, RET_V_W), o_xa.reshape(rows, XA_W), wf, wr, wx, wo, g_ffn[l], wgu, wd,
                g_final, _pick_tile(rows, 256)).reshape(bp, sp, D_MODEL)

    cos_s, s1_s, s2_s = _rotary_tables(jnp.full((bs,), past, jnp.int32))
    (gates_s, fqb_s, fkt_s, fvt_s, _, fkb_s, fvb_s, lft_s, _, rq_s, rk_s, rv_s, rg_s, xq_s) = _inproj(
        x_sample.reshape(1, bs, D_MODEL), g_attn[l], w_main, w_kvt, w_flt, b_f[l], cos_s, s1_s, s2_s, bs)
    two = lambda a: a.reshape(bs, a.shape[-1])
    ckt = cache_fox_k[l].transpose(0, 2, 3, 1)
    cvt = cache_fox_v[l].transpose(0, 2, 3, 1)
    clft = cache_fox_logf[l].transpose(0, 2, 1)
    o_fox_s = _fox_sample(page_table, two(fqb_s).astype(F32), lft_s[0].T, two(fkb_s).astype(F32),
                          two(fvb_s).astype(F32), ckt, cvt, clft, _pick_tile(n_pages, 16))
    o_ret_s, st_s = _ret_sample(two(rq_s), two(rk_s), two(rv_s), two(rg_s), g_ret[l], state_ret[l],
                                _pick_tile(bs, 32))
    o_xa_s = _xattn_sample(two(xq_s), cache_mem_k[l].reshape(bs, n_mem, XA_W),
                           cache_mem_v[l].reshape(bs, n_mem, XA_W))
    y_s = _post(x_sample.reshape(bs, D_MODEL), two(gates_s), o_fox_s, o_ret_s, o_xa_s, wf, wr, wx, wo,
                g_ffn[l], wgu, wd, g_final, bs).reshape(bs, 1, D_MODEL)

    stack = lambda a, shape: a.reshape((1,) + shape)
    sample_major = lambda a_t, n_heads, head_w: (
        a_t.reshape(n_heads, head_w, bs).transpose(2, 0, 1).reshape(1, bs, 1, n_heads, head_w))
    return (y_p, y_s,
            token_major(fkt, FOX_HEADS, FOX_HD), token_major(fvt, FOX_HEADS, FOX_HD),
            lft.transpose(0, 2, 1)[None], stack(st_p, (bp, RET_HEADS, RET_DK, RET_DV)),
            stack(mk_p, (bp, n_mem, XA_HEADS, XA_HD)), stack(mv_p, (bp, n_mem, XA_HEADS, XA_HD)),
            sample_major(fkt_s, FOX_HEADS, FOX_HD), sample_major(fvt_s, FOX_HEADS, FOX_HD),
            lft_s[0].T.reshape(1, bs, 1, FOX_HEADS), stack(st_s, (bs, RET_HEADS, RET_DK, RET_DV)))
```

```python
import functools

import jax
import jax.numpy as jnp
from jax import lax
from jax.experimental import pallas as pl
from jax.experimental.pallas import tpu as pltpu

D_MODEL = 1024
FOX_HEADS = 8
FOX_HD = 64
RET_HEADS = 4
RET_DK = 64
RET_DV = 128
XA_HEADS = 4
XA_HD = 128
FOX_W = FOX_HEADS * FOX_HD
RET_QK_W = RET_HEADS * RET_DK
RET_V_W = RET_HEADS * RET_DV
XA_W = XA_HEADS * XA_HD
D_FF = 2816
RET_CHUNK = 128
ROPE_BASE = 10000.0
EPS = 1e-6
NEG_INF = -1e30

LANES = 128
VMEM_LIMIT = 56 * 1024 * 1024

F32 = jnp.float32
BF16 = jnp.bfloat16

_C_FQ, _C_FK, _C_FV = 0, 512, 1024
_C_RQK, _C_RV, _C_RG, _C_XQ, _C_GATES = 1536, 2048, 2560, 3072, 3584
_W_MAIN = 6656


def _params(sem):
    return pltpu.CompilerParams(dimension_semantics=sem, vmem_limit_bytes=VMEM_LIMIT)


def _dot(a, b):
    return jnp.dot(a, b, preferred_element_type=F32)


def _dot_nt(a, b):
    return lax.dot_general(a, b, (((1,), (1,)), ((), ())), preferred_element_type=F32)


def _dot_tn(a, b):
    return lax.dot_general(a, b, (((0,), (0,)), ((), ())), preferred_element_type=F32)


def _rms(x, g):
    return x * lax.rsqrt(jnp.mean(x * x, axis=-1, keepdims=True) + EPS) * g


def _log_sigmoid(x):
    return -(jnp.maximum(-x, 0.0) + jnp.log1p(jnp.exp(-jnp.abs(x))))


def _sigmoid(x):
    return 1.0 / (1.0 + jnp.exp(-x))


def _split3(x):
    hi = x.astype(BF16)
    r1 = x - hi.astype(F32)
    mid = r1.astype(BF16)
    lo = (r1 - mid.astype(F32)).astype(BF16)
    return hi, mid, lo


def _expand_heads(x, n_heads, head_w):
    width = n_heads * head_w
    lane = lax.broadcasted_iota(jnp.int32, (1, width), 1)
    out = jnp.zeros((1, width), F32)
    for h in range(n_heads):
        sel = (lane >= h * head_w) & (lane < (h + 1) * head_w)
        out = jnp.where(sel, x[:, h:h + 1], out)
    return out


def _inproj_kernel(x_ref, g_ref, wm_ref, wkvt_ref, wflt_ref, bft_ref, cos_ref, s1_ref, s2_ref,
                   gates_ref, fq_ref, fkt_ref, fvt_ref, fktb_ref, fkb_ref, fvb_ref, lft_ref, ct_ref,
                   rq_ref, rk_ref, rv_ref, rg_ref, xq_ref, carry_ref, *, tm):
    j = pl.program_id(1)
    hn = _rms(x_ref[0], g_ref[...]).astype(BF16)

    def proj(c0, width):
        return _dot(hn, wm_ref[:, c0:c0 + width])

    fq_ref[0] = (proj(_C_FQ, FOX_W) * (FOX_HD ** -0.5)).astype(BF16)
    fkb_ref[0] = proj(_C_FK, FOX_W).astype(BF16)
    fvb_ref[0] = proj(_C_FV, FOX_W).astype(BF16)
    fkt = _dot_nt(wkvt_ref[:FOX_W, :], hn)
    fkt_ref[0] = fkt
    fktb_ref[0] = fkt.astype(BF16)
    fvt_ref[0] = _dot_nt(wkvt_ref[FOX_W:, :], hn)

    rqk = proj(_C_RQK, 2 * RET_QK_W)
    cos, s1, s2 = cos_ref[...], s1_ref[...], s2_ref[...]
    for ref, off in ((rq_ref, 0), (rk_ref, RET_QK_W)):
        r = rqk[:, off:off + RET_QK_W]
        ref[0] = (r * cos + pltpu.roll(r, RET_QK_W - RET_DK // 2, 1) * s1
                  + pltpu.roll(r, RET_DK // 2, 1) * s2)
    rv_ref[0] = proj(_C_RV, RET_V_W)
    rg_ref[0] = proj(_C_RG, RET_V_W)
    xq_ref[0] = proj(_C_XQ, XA_W)
    for c in range(0, 3 * D_MODEL, 512):
        gates_ref[0, :, c:c + 512] = proj(_C_GATES + c, 512)

    lft = _log_sigmoid(_dot_nt(wflt_ref[...], hn) + bft_ref[...])
    lft_ref[0] = lft

    @pl.when(j == 0)
    def _():
        carry_ref[...] = jnp.zeros_like(carry_ref)

    row = lax.broadcasted_iota(jnp.int32, (tm, tm), 0)
    col = lax.broadcasted_iota(jnp.int32, (tm, tm), 1)
    tri = (row <= col).astype(BF16)
    hi, mid, lo = _split3(lft)
    ct = _dot(hi, tri) + _dot(mid, tri) + _dot(lo, tri) + carry_ref[:, 0:1]
    ct_ref[0] = ct
    carry_ref[...] = jnp.broadcast_to(ct[:, tm - 1:tm], carry_ref.shape)


def _inproj(x, g_attn, w_main, w_kvt, w_flt, b_f, cos, s1, s2, tm):
    b, s, _ = x.shape
    grid = (b, s // tm)
    tok = lambda w: pl.BlockSpec((1, tm, w), lambda i, j: (i, j, 0))
    feat = lambda h: pl.BlockSpec((1, h, tm), lambda i, j: (i, 0, j))
    const = lambda shape: pl.BlockSpec(shape, lambda i, j: (0,) * len(shape))
    tab = pl.BlockSpec((tm, RET_QK_W), lambda i, j: (j, 0))
    outs = [
        (tok, 3 * D_MODEL, F32), (tok, FOX_W, BF16),
        (feat, FOX_W, F32), (feat, FOX_W, F32), (feat, FOX_W, BF16),
        (tok, FOX_W, BF16), (tok, FOX_W, BF16),
        (feat, FOX_HEADS, F32), (feat, FOX_HEADS, F32),
        (tok, RET_QK_W, F32), (tok, RET_QK_W, F32), (tok, RET_V_W, F32), (tok, RET_V_W, F32), (tok, XA_W, F32),
    ]
    shape = lambda kind, w: (b, s, w) if kind is tok else (b, w, s)
    return pl.pallas_call(
        functools.partial(_inproj_kernel, tm=tm),
        grid=grid,
        in_specs=[tok(D_MODEL), const((1, D_MODEL)), const((D_MODEL, _W_MAIN)), const((2 * FOX_W, D_MODEL)),
                  const((FOX_HEADS, D_MODEL)), const((FOX_HEADS, 1)), tab, tab, tab],
        out_specs=[kind(w) for kind, w, _ in outs],
        out_shape=[jax.ShapeDtypeStruct(shape(kind, w), dt) for kind, w, dt in outs],
        scratch_shapes=[pltpu.VMEM((FOX_HEADS, LANES), F32)],
        compiler_params=_params(("arbitrary", "arbitrary")),
        name="inproj",
    )(x, g_attn.reshape(1, D_MODEL), w_main, w_kvt, w_flt, b_f.reshape(FOX_HEADS, 1), cos, s1, s2)


def _lane_blocks(x, op):
    out = x[:, :LANES]
    for c in range(LANES, x.shape[1], LANES):
        out = op(out, x[:, c:c + LANES])
    return out


_LOG2E = 1.4426950408889634


def _fox_prompt_kernel(q_ref, kt_ref, v_ref, ct_ref, o_ref, s_ref, mx_ref, l_ref, acc_ref, *, t):
    i = pl.program_id(2)
    lane = lax.broadcasted_iota(jnp.int32, (t, LANES), 1)
    row = lax.broadcasted_iota(jnp.int32, (t, t), 0)
    col = lax.broadcasted_iota(jnp.int32, (t, t), 1)
    chains = [(hh, half) for hh in range(2) for half in range(2)]
    qm = []
    for hh, half in chains:
        q = q_ref[0, half * t:(half + 1) * t, :]
        head_lanes = (lane >= hh * FOX_HD) & (lane < (hh + 1) * FOX_HD)
        qm.append(jnp.where(head_lanes, q, jnp.zeros_like(q)))

    def score_tile(c, j, diag):
        hh = chains[c][0]
        off = pl.multiple_of(j * t, t)
        s = (_dot(qm[c], kt_ref[0, :, pl.ds(off, t)]) - ct_ref[0, 0, hh:hh + 1, pl.ds(off, t)]) * _LOG2E
        if diag:
            s = jnp.where(col <= row, s, NEG_INF)
        s_ref[c, j] = s
        mx_ref[c] = jnp.maximum(mx_ref[c], _lane_blocks(s, jnp.maximum))

    tail = {0: ((0, True),), 1: ((0, False), (1, True))}

    mx_ref[...] = jnp.full_like(mx_ref, NEG_INF)

    def pass1(j, carry):
        for c in range(4):
            score_tile(c, j, False)
        return carry

    lax.fori_loop(0, 2 * i, pass1, 0)
    for c, (_, half) in enumerate(chains):
        for dj, diag in tail[half]:
            score_tile(c, 2 * i + dj, diag)

    m = [jnp.max(mx_ref[c], axis=1, keepdims=True) for c in range(4)]
    l_ref[...] = jnp.zeros_like(l_ref)
    acc_ref[...] = jnp.zeros_like(acc_ref)

    def value_tile(c, j):
        off = pl.multiple_of(j * t, t)
        p = jnp.exp2(s_ref[c, j] - m[c])
        l_ref[c] += _lane_blocks(p, jnp.add)
        acc_ref[c] += _dot(p.astype(BF16), v_ref[0, pl.ds(off, t), :])

    def pass2(j, carry):
        for c in range(4):
            value_tile(c, j)
        return carry

    lax.fori_loop(0, 2 * i, pass2, 0)
    for c, (_, half) in enumerate(chains):
        for dj, _ in tail[half]:
            value_tile(c, 2 * i + dj)

    out = [acc_ref[c] / jnp.sum(l_ref[c], axis=1, keepdims=True) for c in range(4)]
    for half in range(2):
        o_ref[0, half * t:(half + 1) * t, :] = jnp.where(lane < FOX_HD, out[half], out[2 + half])


def _fox_prompt(fqb, fktb, fvb, ct, t):
    b, s, _ = fqb.shape
    pairs = FOX_HEADS // 2
    ct4 = ct.reshape(b, pairs, 2, s)
    return pl.pallas_call(
        functools.partial(_fox_prompt_kernel, t=t),
        grid=(b, pairs, s // (2 * t)),
        in_specs=[pl.BlockSpec((1, 2 * t, LANES), lambda bi, pr, i: (bi, i, pr)),
                  pl.BlockSpec((1, LANES, s), lambda bi, pr, i: (bi, pr, 0)),
                  pl.BlockSpec((1, s, LANES), lambda bi, pr, i: (bi, 0, pr)),
                  pl.BlockSpec((1, 1, 2, s), lambda bi, pr, i: (bi, pr, 0, 0))],
        out_specs=pl.BlockSpec((1, 2 * t, LANES), lambda bi, pr, i: (bi, i, pr)),
        out_shape=jax.ShapeDtypeStruct((b, s, FOX_W), F32),
        scratch_shapes=[pltpu.VMEM((4, s // t, t, t), F32), pltpu.VMEM((4, t, LANES), F32),
                        pltpu.VMEM((4, t, LANES), F32), pltpu.VMEM((4, t, LANES), F32)],
        compiler_params=_params(("arbitrary", "arbitrary", "arbitrary")),
        name="fox_prompt",
    )(fqb, fktb, fvb, ct4)


def _ret_prompt_kernel(q_ref, k_ref, v_ref, rg_ref, gr_ref, dmask_ref, qdec_ref, kdec_ref, cdec_ref,
                       o_ref, st_ref):
    n = pl.program_id(1)

    @pl.when(n == 0)
    def _():
        st_ref[...] = jnp.zeros_like(st_ref)

    c = RET_CHUNK
    lane = lax.broadcasted_iota(jnp.int32, (c, LANES), 1)
    for bi in range(q_ref.shape[0]):
        for pr in range(RET_HEADS // 2):
            q = q_ref[bi, :, pr * LANES:(pr + 1) * LANES]
            k = k_ref[bi, :, pr * LANES:(pr + 1) * LANES] * (RET_DK ** -0.5)
            state = st_ref[bi, pr]
            qd = (q * qdec_ref[pr]).astype(BF16)
            kd = k * kdec_ref[pr]
            kb = k.astype(BF16)
            new_state = cdec_ref[pr] * state
            state_b = state.astype(BF16)
            for hh in range(2):
                vs = slice((2 * pr + hh) * RET_DV, (2 * pr + hh + 1) * RET_DV)
                head_lanes = (lane >= hh * RET_DK) & (lane < (hh + 1) * RET_DK)
                vb = v_ref[bi, :, vs].astype(BF16)
                qm = jnp.where(head_lanes, q, 0.0).astype(BF16)
                inner = _dot_nt(qm, kb) * dmask_ref[pr, hh]
                qdm = jnp.where(head_lanes, qd, jnp.zeros_like(qd))
                o = _dot(inner.astype(BF16), vb) + _dot(qdm, state_b)
                kdm = jnp.where(head_lanes, kd, 0.0).astype(BF16)
                new_state = new_state + _dot_tn(kdm, vb)
                mu = jnp.mean(o, axis=-1, keepdims=True)
                var = jnp.mean(jnp.square(o - mu), axis=-1, keepdims=True)
                y = (o - mu) * lax.rsqrt(var + EPS) * gr_ref[:, vs]
                rg = rg_ref[bi, :, vs]
                o_ref[bi, :, vs] = rg * _sigmoid(rg) * y
            st_ref[bi, pr] = new_state


def _ret_tables(length):
    h = RET_HEADS
    log_g = jnp.log(1.0 - 2.0 ** (-5.0 - jnp.arange(h, dtype=F32)))
    i = jnp.arange(length, dtype=F32)
    diff = i[:, None] - i[None, :]
    dmask = jnp.where(diff[None] >= 0, jnp.exp(jnp.maximum(diff, 0.0)[None] * log_g[:, None, None]), 0.0)
    q_dec = jnp.exp((i + 1.0)[:, None] * log_g[None, :])
    k_dec = jnp.exp((length - 1.0 - i)[:, None] * log_g[None, :])
    chunk_dec = jnp.exp(length * log_g)
    return dmask, q_dec, k_dec, chunk_dec


def _ret_prompt(rq, rk, rv, rg, g_ret):
    b, s, _ = rq.shape
    c = RET_CHUNK
    pairs = RET_HEADS // 2
    dmask, q_dec, k_dec, chunk_dec = _ret_tables(c)
    dmask = dmask.reshape(pairs, 2, c, c)
    qdec = jnp.repeat(q_dec, RET_DK, axis=1).reshape(c, pairs, LANES).transpose(1, 0, 2)
    kdec = jnp.repeat(k_dec, RET_DK, axis=1).reshape(c, pairs, LANES).transpose(1, 0, 2)
    cdec = jnp.repeat(chunk_dec, RET_DK).reshape(pairs, LANES, 1)
    bb = _pick_tile(b, 8)
    tok = lambda w: pl.BlockSpec((bb, c, w), lambda bi, n: (bi, n, 0))
    const = lambda shape: pl.BlockSpec(shape, lambda bi, n: (0,) * len(shape))
    o, st = pl.pallas_call(
        _ret_prompt_kernel,
        grid=(b // bb, s // c),
        in_specs=[tok(RET_QK_W), tok(RET_QK_W), tok(RET_V_W), tok(RET_V_W), const((1, RET_V_W)),
                  const((pairs, 2, c, c)), const((pairs, c, LANES)), const((pairs, c, LANES)),
                  const((pairs, LANES, 1))],
        out_specs=[tok(RET_V_W),
                   pl.BlockSpec((bb, pairs, 2 * RET_DK, RET_DV), lambda bi, n: (bi, 0, 0, 0))],
        out_shape=[jax.ShapeDtypeStruct((b, s, RET_V_W), F32),
                   jax.ShapeDtypeStruct((b, pairs, 2 * RET_DK, RET_DV), F32)],
        compiler_params=_params(("arbitrary", "arbitrary")),
        name="ret_prompt",
    )(rq, rk, rv, rg, g_ret.reshape(1, RET_V_W), dmask, qdec, kdec, cdec)
    return o, st.reshape(b, RET_HEADS, RET_DK, RET_DV)


def _ret_sample_kernel(q_ref, k_ref, v_ref, rg_ref, gr_ref, qdec_ref, cdec_ref, s0_ref, o_ref, s1_ref, *, bt):
    rows = RET_HEADS * RET_DK
    q = q_ref[...]
    k = k_ref[...] * (RET_DK ** -0.5)
    qg = q * qdec_ref[...]
    qk = q * k
    for h in range(RET_HEADS):
        v = v_ref[:, h * RET_DV:(h + 1) * RET_DV]
        cd = cdec_ref[:, h * RET_DV:(h + 1) * RET_DV]
        inner = jnp.sum(qk[:, h * RET_DK:(h + 1) * RET_DK], axis=1, keepdims=True)
        o = inner * v
        for d in range(RET_DK):
            r = h * RET_DK + d
            srow = s0_ref[pl.ds(r, bt, stride=rows), :]
            o = o + qg[:, r:r + 1] * srow
            s1_ref[pl.ds(r, bt, stride=rows), :] = cd * srow + k[:, r:r + 1] * v
        mu = jnp.mean(o, axis=-1, keepdims=True)
        var = jnp.mean(jnp.square(o - mu), axis=-1, keepdims=True)
        y = (o - mu) * lax.rsqrt(var + EPS) * gr_ref[:, h * RET_DV:(h + 1) * RET_DV]
        rg = rg_ref[:, h * RET_DV:(h + 1) * RET_DV]
        o_ref[:, h * RET_DV:(h + 1) * RET_DV] = rg * _sigmoid(rg) * y


def _ret_sample(rq, rk, rv, rg, g_ret, s0, bt):
    t = rq.shape[0]
    rows = RET_HEADS * RET_DK
    _, q_dec, _, chunk_dec = _ret_tables(1)
    qdec = jnp.repeat(q_dec, RET_DK, axis=1).reshape(1, RET_QK_W)
    cdec = jnp.repeat(chunk_dec, RET_DV).reshape(1, RET_V_W)
    tok = lambda w: pl.BlockSpec((bt, w), lambda i: (i, 0))
    const = lambda w: pl.BlockSpec((1, w), lambda i: (0, 0))
    st = pl.BlockSpec((bt * rows, RET_DV), lambda i: (i, 0))
    o, s1 = pl.pallas_call(
        functools.partial(_ret_sample_kernel, bt=bt),
        grid=(t // bt,),
        in_specs=[tok(RET_QK_W), tok(RET_QK_W), tok(RET_V_W), tok(RET_V_W), const(RET_V_W),
                  const(RET_QK_W), const(RET_V_W), st],
        out_specs=[tok(RET_V_W), st],
        out_shape=[jax.ShapeDtypeStruct((t, RET_V_W), F32), jax.ShapeDtypeStruct((t * rows, RET_DV), F32)],
        compiler_params=_params(("arbitrary",)),
        name="ret_sample",
    )(rq, rk, rv, rg, g_ret.reshape(1, RET_V_W), qdec, cdec, s0.reshape(t * rows, RET_DV))
    return o, s1.reshape(t, RET_HEADS, RET_DK, RET_DV)


def _xattn_prompt_kernel(q_ref, mk_ref, mv_ref, o_ref):
    scale = XA_HD ** -0.5
    for h in range(XA_HEADS):
        sl = slice(h * XA_HD, (h + 1) * XA_HD)
        s = _dot_nt(q_ref[0, :, sl].astype(BF16), mk_ref[0, :, sl].astype(BF16)) * scale
        p = jnp.exp(s - jnp.max(s, axis=1, keepdims=True))
        p = p / jnp.sum(p, axis=1, keepdims=True)
        o_ref[0, :, sl] = _dot(p.astype(BF16), mv_ref[0, :, sl].astype(BF16))


def _xattn_prompt(xq, mk, mv, t):
    b, s, _ = xq.shape
    m = mk.shape[1]
    return pl.pallas_call(
        _xattn_prompt_kernel,
        grid=(b, s // t),
        in_specs=[pl.BlockSpec((1, t, XA_W), lambda bi, i: (bi, i, 0)),
                  pl.BlockSpec((1, m, XA_W), lambda bi, i: (bi, 0, 0)),
                  pl.BlockSpec((1, m, XA_W), lambda bi, i: (bi, 0, 0))],
        out_specs=pl.BlockSpec((1, t, XA_W), lambda bi, i: (bi, i, 0)),
        out_shape=jax.ShapeDtypeStruct((b, s, XA_W), F32),
        compiler_params=_params(("arbitrary", "arbitrary")),
        name="xattn_prompt",
    )(xq, mk, mv)


def _head_query_block(q_row, n_cols, head_w):
    w = q_row.shape[1]
    r = lax.broadcasted_iota(jnp.int32, (w, w), 0)
    c = lax.broadcasted_iota(jnp.int32, (w, w), 1)
    diag = jnp.where(r == c, jnp.broadcast_to(q_row, (w, w)), 0.0)
    rr = lax.broadcasted_iota(jnp.int32, (w, n_cols), 0)
    cc = lax.broadcasted_iota(jnp.int32, (w, n_cols), 1)
    ones = ((rr >= cc * head_w) & (rr < (cc + 1) * head_w)).astype(F32)
    return _dot(diag, ones)


def _head_expand_matrix(n_rows, n_heads, head_w):
    width = n_heads * head_w
    rr = lax.broadcasted_iota(jnp.int32, (n_rows, width), 0)
    cc = lax.broadcasted_iota(jnp.int32, (n_rows, width), 1)
    return ((cc >= rr * head_w) & (cc < (rr + 1) * head_w)).astype(F32)


_HPAD = 8


def _xattn_sample_kernel(q_ref, mk_ref, mv_ref, o_ref):
    qblk = _head_query_block(q_ref[0], _HPAD, XA_HD)
    s = _dot(mk_ref[0], qblk) * (XA_HD ** -0.5)
    m = jnp.max(s, axis=0, keepdims=True)
    p = jnp.exp(s - m)
    l = jnp.sum(p, axis=0, keepdims=True)
    pe = _dot(p, _head_expand_matrix(_HPAD, XA_HEADS, XA_HD))
    acc = jnp.sum(pe * mv_ref[0], axis=0, keepdims=True)
    o_ref[0] = acc / _expand_heads(l, XA_HEADS, XA_HD)


def _xattn_sample(xq, mk, mv):
    t, m, _ = mk.shape
    o = pl.pallas_call(
        _xattn_sample_kernel,
        grid=(t,),
        in_specs=[pl.BlockSpec((1, 1, XA_W), lambda i: (i, 0, 0)),
                  pl.BlockSpec((1, m, XA_W), lambda i: (i, 0, 0)),
                  pl.BlockSpec((1, m, XA_W), lambda i: (i, 0, 0))],
        out_specs=pl.BlockSpec((1, 1, XA_W), lambda i: (i, 0, 0)),
        out_shape=jax.ShapeDtypeStruct((t, 1, XA_W), F32),
        compiler_params=_params(("arbitrary",)),
        name="xattn_sample",
    )(xq.reshape(t, 1, XA_W), mk, mv)
    return o.reshape(t, XA_W)


def _suffix_sum_lanes(x):
    n = x.shape[1]
    lane = lax.broadcasted_iota(jnp.int32, x.shape, 1)
    sh = 1
    while sh < n:
        x = x + jnp.where(lane < n - sh, pltpu.roll(x, n - sh, 1), 0.0)
        sh *= 2
    return x


def _fox_sample_kernel(pt_ref, q_ref, cn_ref, kn_ref, vn_ref, ck_hbm, clf_hbm, cv_hbm, o_ref,
                       buf_ref, lfbuf_ref, sem, lfsem, qb_ref, part_ref, p_ref, s_ref, m_ref, sn_ref, cnb_ref,
                       carry_ref, acc_ref, l_ref, *, pps, n_steps, n_pages, n_tok):
    hrow = lax.broadcasted_iota(jnp.int32, (FOX_HEADS, FOX_W), 0)
    hlane = lax.broadcasted_iota(jnp.int32, (FOX_HEADS, FOX_W), 1)
    head_mask = (hlane >= hrow * FOX_HD) & (hlane < (hrow + 1) * FOX_HD)

    def rows_to_col(row_vals):
        r8 = lax.broadcasted_iota(jnp.int32, (FOX_HEADS, FOX_HEADS), 0)
        c8 = lax.broadcasted_iota(jnp.int32, (FOX_HEADS, FOX_HEADS), 1)
        b = jnp.broadcast_to(row_vals, (FOX_HEADS, FOX_HEADS))
        return jnp.sum(jnp.where(r8 == c8, b, 0.0), axis=1, keepdims=True)

    def col_to_head_row(col_vals):
        b = jnp.broadcast_to(col_vals, (FOX_HEADS, FOX_W))
        return jnp.sum(jnp.where(head_mask, b, 0.0), axis=0, keepdims=True)

    def unrolled(n, body, init=0):
        return lax.fori_loop(0, n, body, init, unroll=True)

    def page_copies(keys, b, j, slot, i):
        page = 0 if b is None else pt_ref[b, n_pages - 1 - (j * pps + i)]
        if keys:
            return (pltpu.make_async_copy(ck_hbm.at[page], buf_ref.at[slot, i], sem.at[slot]),
                    pltpu.make_async_copy(clf_hbm.at[page], lfbuf_ref.at[slot, i], lfsem.at[slot]))
        return (pltpu.make_async_copy(cv_hbm.at[page], buf_ref.at[slot, i], sem.at[slot]),)

    def start_chunk(keys, b, j, slot):
        def body(i, c):
            for cp in page_copies(keys, b, j, slot, i):
                cp.start()
            return c
        unrolled(pps, body)

    def wait_chunk(keys, slot):
        def body(i, c):
            for cp in page_copies(keys, None, 0, slot, i):
                cp.wait()
            return c
        unrolled(pps, body)

    def key_chunk(b, j, slot):
        @pl.when(j == 0)
        def _():
            q = q_ref[pl.ds(b, 1), :]
            qb_ref[...] = jnp.transpose(jnp.broadcast_to(q, (LANES, FOX_W)))
            cnb_ref[...] = jnp.broadcast_to(rows_to_col(cn_ref[pl.ds(b, 1), :]), cnb_ref.shape)
            carry_ref[...] = jnp.zeros_like(carry_ref)
            m_ref[...] = jnp.full_like(m_ref, NEG_INF)

        def head(h, c):
            qbh = qb_ref[pl.ds(pl.multiple_of(h * FOX_HD, FOX_HD), FOX_HD), :]

            def page_part(i, c2):
                prod = buf_ref[slot, i, h] * qbh
                part_ref[i, h] = jnp.sum(prod.reshape(FOX_HD // 8, 8, LANES), axis=0)
                return c2
            return unrolled(pps, page_part, c)
        unrolled(FOX_HEADS, head)

        cnb = cnb_ref[...]

        def page_scores(i, mc):
            m, carry = mc
            lf = lfbuf_ref[slot, i]
            suf = _suffix_sum_lanes(lf)
            s = jnp.sum(part_ref[i], axis=1) + cnb + ((suf - lf) + carry)
            s_ref[j * pps + i] = s
            return jnp.maximum(m, s), carry + suf[:, 0:1]
        m, carry = unrolled(pps, page_scores, (m_ref[...], carry_ref[...]))
        carry_ref[...] = carry
        m_ref[...] = m

        @pl.when(j == n_steps - 1)
        def _():
            q8 = jnp.where(head_mask, jnp.broadcast_to(q_ref[pl.ds(b, 1), :], (FOX_HEADS, FOX_W)), 0.0)
            sn = jnp.sum(q8 * kn_ref[pl.ds(b, 1), :], axis=1, keepdims=True)
            mf = jnp.maximum(jnp.max(m, axis=1, keepdims=True), sn)
            sn_ref[...] = jnp.broadcast_to(sn, sn_ref.shape)
            m_ref[...] = jnp.broadcast_to(mf, m_ref.shape)

    def value_chunk(b, j, slot):
        @pl.when(j == 0)
        def _():
            acc_ref[...] = jnp.zeros_like(acc_ref)
            l_ref[...] = jnp.zeros_like(l_ref)

        m = m_ref[...]

        def page_probs(i, l):
            p = jnp.exp(s_ref[j * pps + i] - m)
            p_ref[i] = p
            return l + p
        l = unrolled(pps, page_probs, l_ref[...])
        l_ref[...] = l

        def head(h, c):
            def page_acc(i, a):
                return a + p_ref[i, pl.ds(h, 1), :] * buf_ref[slot, i, h]
            acc_ref[h] = unrolled(pps, page_acc, acc_ref[h])
            return c
        unrolled(FOX_HEADS, head)

        @pl.when(j == n_steps - 1)
        def _():
            pn = jnp.exp(sn_ref[:, 0:1] - m[:, 0:1])
            lsum = jnp.sum(l, axis=1, keepdims=True) + pn
            acc_t = jnp.transpose(acc_ref[...].reshape(FOX_W, LANES))
            acc = jnp.sum(acc_t, axis=0, keepdims=True) + col_to_head_row(pn) * vn_ref[pl.ds(b, 1), :]
            o_ref[pl.ds(b, 1), :] = acc / col_to_head_row(lsum)

    start_chunk(True, 0, 0, 0)

    def token(b, carry):
        def key_step(j, c):
            slot = j % 2
            wait_chunk(True, slot)

            @pl.when(j + 1 < n_steps)
            def _():
                start_chunk(True, b, j + 1, 1 - slot)

            @pl.when(j + 1 == n_steps)
            def _():
                start_chunk(False, b, 0, 1 - slot)
            key_chunk(b, j, slot)
            return c

        def value_step(j, c):
            slot = (n_steps + j) % 2
            wait_chunk(False, slot)

            @pl.when(j + 1 < n_steps)
            def _():
                start_chunk(False, b, j + 1, 1 - slot)

            @pl.when((j + 1 == n_steps) & (b + 1 < n_tok))
            def _():
                start_chunk(True, b + 1, 0, 1 - slot)
            value_chunk(b, j, slot)
            return c

        unrolled(n_steps, key_step)
        unrolled(n_steps, value_step)
        return carry

    lax.fori_loop(0, n_tok, token, 0)


def _fox_sample(page_table, fq, logf, fk, fv, ckt, cvt, clft, pps):
    t, n_pages = page_table.shape
    page = ckt.shape[3]
    assert page == LANES
    n_steps = n_pages // pps
    full = lambda w: pl.BlockSpec((t, w), lambda i, pt: (0, 0))
    hbm = pl.BlockSpec(memory_space=pl.ANY)
    hb = (FOX_HEADS, LANES)
    grid_spec = pltpu.PrefetchScalarGridSpec(
        num_scalar_prefetch=1,
        grid=(1,),
        in_specs=[full(FOX_W), full(FOX_HEADS), full(FOX_W), full(FOX_W), hbm, hbm, hbm],
        out_specs=full(FOX_W),
        scratch_shapes=[pltpu.VMEM((2, pps, FOX_HEADS, FOX_HD, LANES), F32),
                        pltpu.VMEM((2, pps) + hb, F32),
                        pltpu.SemaphoreType.DMA((2,)), pltpu.SemaphoreType.DMA((2,)),
                        pltpu.VMEM((FOX_W, LANES), F32),
                        pltpu.VMEM((pps, FOX_HEADS, 8, LANES), F32),
                        pltpu.VMEM((pps,) + hb, F32),
                        pltpu.VMEM((n_pages,) + hb, F32),
                        pltpu.VMEM(hb, F32), pltpu.VMEM(hb, F32), pltpu.VMEM(hb, F32), pltpu.VMEM(hb, F32),
                        pltpu.VMEM((FOX_HEADS, FOX_HD, LANES), F32), pltpu.VMEM(hb, F32)],
    )
    return pl.pallas_call(
        functools.partial(_fox_sample_kernel, pps=pps, n_steps=n_steps, n_pages=n_pages, n_tok=t),
        grid_spec=grid_spec,
        out_shape=jax.ShapeDtypeStruct((t, FOX_W), F32),
        compiler_params=_params(("arbitrary",)),
        name="fox_sample",
    )(page_table, fq, logf, fk, fv, ckt, clft, cvt)


def _memkv_kernel(x_ref, g_ref, w_ref, mk_ref, mv_ref):
    hn = _rms(x_ref[...], g_ref[...]).astype(BF16)
    mk_ref[...] = _dot(hn, w_ref[:, :XA_W])
    mv_ref[...] = _dot(hn, w_ref[:, XA_W:])


def _memkv(mem, g_mem, w_kv, tm):
    rows = mem.shape[0]
    return pl.pallas_call(
        _memkv_kernel,
        grid=(rows // tm,),
        in_specs=[pl.BlockSpec((tm, D_MODEL), lambda i: (i, 0)),
                  pl.BlockSpec((1, D_MODEL), lambda i: (0, 0)),
                  pl.BlockSpec((D_MODEL, 2 * XA_W), lambda i: (0, 0))],
        out_specs=[pl.BlockSpec((tm, XA_W), lambda i: (i, 0))] * 2,
        out_shape=[jax.ShapeDtypeStruct((rows, XA_W), F32)] * 2,
        compiler_params=_params(("arbitrary",)),
        name="memkv",
    )(mem, g_mem.reshape(1, D_MODEL), w_kv)


_FF_CHUNK = 256


def _post_kernel(x_ref, gates_ref, of_ref, or_ref, ox_ref, wf_ref, wr_ref, wx_ref, wo_ref, gffn_ref,
                 wgu_ref, wd_ref, gfin_ref, y_ref):
    mix = None
    for idx, (o_ref, w_ref) in enumerate(((of_ref, wf_ref), (or_ref, wr_ref), (ox_ref, wx_ref))):
        gate = _sigmoid(gates_ref[:, idx * D_MODEL:(idx + 1) * D_MODEL])
        term = gate * _dot(o_ref[...].astype(BF16), w_ref[...])
        mix = term if mix is None else mix + term
    x = x_ref[...] + _dot(mix.astype(BF16), wo_ref[...])
    hb = _rms(x, gffn_ref[...]).astype(BF16)
    ffn = jnp.zeros_like(x)
    for c in range(0, D_FF, _FF_CHUNK):
        u_gate = _dot(hb, wgu_ref[:, c:c + _FF_CHUNK])
        u_up = _dot(hb, wgu_ref[:, D_FF + c:D_FF + c + _FF_CHUNK])
        act = (u_gate * _sigmoid(u_gate) * u_up).astype(BF16)
        ffn = ffn + _dot(act, wd_ref[c:c + _FF_CHUNK, :])
    y_ref[...] = _rms(x + ffn, gfin_ref[...])


def _post(x, gates, o_fox, o_ret, o_xa, wf, wr, wx, wo, g_ffn, wgu, wd, g_final, tm):
    rows = x.shape[0]
    row = lambda w: pl.BlockSpec((tm, w), lambda i: (i, 0))
    const = lambda a, b: pl.BlockSpec((a, b), lambda i: (0, 0))
    return pl.pallas_call(
        _post_kernel,
        grid=(rows // tm,),
        in_specs=[row(D_MODEL), row(3 * D_MODEL), row(FOX_W), row(RET_V_W), row(XA_W),
                  const(FOX_W, D_MODEL), const(RET_V_W, D_MODEL), const(XA_W, D_MODEL),
                  const(D_MODEL, D_MODEL), const(1, D_MODEL), const(D_MODEL, 2 * D_FF),
                  const(D_FF, D_MODEL), const(1, D_MODEL)],
        out_specs=row(D_MODEL),
        out_shape=jax.ShapeDtypeStruct((rows, D_MODEL), F32),
        compiler_params=_params(("arbitrary",)),
        name="post",
    )(x, gates, o_fox, o_ret, o_xa, wf, wr, wx, wo, g_ffn.reshape(1, D_MODEL), wgu, wd,
      g_final.reshape(1, D_MODEL))


def _rotary_tables(pos):
    half = RET_DK // 2
    inv = ROPE_BASE ** (-jnp.arange(half, dtype=F32) / half)
    ang = pos.astype(F32)[:, None] * inv[None, :]
    cos, sin = jnp.cos(ang), jnp.sin(ang)
    zero = jnp.zeros_like(sin)
    tile = lambda a, b: jnp.tile(jnp.concatenate([a, b], axis=1), (1, RET_HEADS))
    return tile(cos, cos), tile(-sin, zero), tile(zero, sin)


def _pick_tile(n, pref):
    t = min(n, pref)
    while n % t:
        t //= 2
    return t


def kernel(x_prompt, x_sample, mem_prompt, cache_fox_k, cache_fox_v, cache_fox_logf, state_ret, cache_mem_k, cache_mem_v, page_table, g_attn, w_in, b_f, g_ret, w_br_fox, w_br_ret, w_br_xa, w_o, g_ffn, w_gu, w_down, g_mem, w_mem_kv, g_final):
    bp, sp, _ = x_prompt.shape
    bs, ts, _ = x_sample.shape
    depth = w_in.shape[0]
    assert depth == 1 and ts == 1
    n_pages, page = page_table.shape[1], cache_fox_k.shape[2]
    past = n_pages * page
    n_mem = mem_prompt.shape[1]
    l = 0

    flog0 = 3 * FOX_W
    w_l = w_in[l]
    w_main = jnp.concatenate([w_l[:, :flog0], w_l[:, flog0 + FOX_HEADS:]], axis=1).astype(BF16)
    w_kvt = w_l[:, FOX_W:flog0].T.astype(BF16)
    w_flt = w_l[:, flog0:flog0 + FOX_HEADS].T.astype(BF16)
    wf, wr, wx = w_br_fox[l].astype(BF16), w_br_ret[l].astype(BF16), w_br_xa[l].astype(BF16)
    wo, wgu, wd = w_o[l].astype(BF16), w_gu[l].astype(BF16), w_down[l].astype(BF16)
    w_kv = w_mem_kv[l].astype(BF16)

    def token_major(a_t, n_heads, head_w):
        b, _, s = a_t.shape
        return a_t.reshape(b, n_heads, head_w, s).transpose(0, 3, 1, 2)[None]

    mk_p, mv_p = _memkv(mem_prompt.reshape(bp * n_mem, D_MODEL), g_mem[l], w_kv, _pick_tile(bp * n_mem, 256))
    mk_p = mk_p.reshape(bp, n_mem, XA_W)
    mv_p = mv_p.reshape(bp, n_mem, XA_W)
    tm = _pick_tile(sp, 256)
    cos, s1, s2 = _rotary_tables(jnp.arange(sp))
    (gates, fqb, fkt, fvt, fktb, _, fvb, lft, ct, rq, rk, rv, rg, xq) = _inproj(
        x_prompt, g_attn[l], w_main, w_kvt, w_flt, b_f[l], cos, s1, s2, tm)
    o_fox = _fox_prompt(fqb, fktb, fvb, ct, _pick_tile(sp // 2, 512))
    o_ret, st_p = _ret_prompt(rq, rk, rv, rg, g_ret[l])
    o_xa = _xattn_prompt(xq, mk_p, mv_p, _pick_tile(sp, 256))
    rows = bp * sp
    y_p = _post(x_prompt.reshape(rows, D_MODEL), gates.reshape(rows, 3 * D_MODEL), o_fox.reshape(rows, FOX_W),
                o_ret.reshape(rows, RET_V_W), o_xa.reshape(rows, XA_W), wf, wr, wx, wo, g_ffn[l], wgu, wd,
                g_final, _pick_tile(rows, 256)).reshape(bp, sp, D_MODEL)

    cos_s, s1_s, s2_s = _rotary_tables(jnp.full((bs,), past, jnp.int32))
    (gates_s, fqb_s, fkt_s, fvt_s, _, fkb_s, fvb_s, lft_s, _, rq_s, rk_s, rv_s, rg_s, xq_s) = _inproj(
        x_sample.reshape(1, bs, D_MODEL), g_attn[l], w_main, w_kvt, w_flt, b_f[l], cos_s, s1_s, s2_s, bs)
    two = lambda a: a.reshape(bs, a.shape[-1])
    ckt = cache_fox_k[l].transpose(0, 2, 3, 1)
    cvt = cache_fox_v[l].transpose(0, 2, 3, 1)
    clft = cache_fox_logf[l].transpose(0, 2, 1)
    o_fox_s = _fox_sample(page_table, two(fqb_s).astype(F32), lft_s[0].T, two(fkb_s).astype(F32),
                          two(fvb_s).astype(F32), ckt, cvt, clft, _pick_tile(n_pages, 16))
    o_ret_s, st_s = _ret_sample(two(rq_s), two(rk_s), two(rv_s), two(rg_s), g_ret[l], state_ret[l],
                                _pick_tile(bs, 32))
    o_xa_s = _xattn_sample(two(xq_s), cache_mem_k[l].reshape(bs, n_mem, XA_W),
                           cache_mem_v[l].reshape(bs, n_mem, XA_W))
    y_s = _post(x_sample.reshape(bs, D_MODEL), two(gates_s), o_fox_s, o_ret_s, o_xa_s, wf, wr, wx, wo,
                g_ffn[l], wgu, wd, g_final, bs).reshape(bs, 1, D_MODEL)

    stack = lambda a, shape: a.reshape((1,) + shape)
    sample_major = lambda a_t, n_heads, head_w: (
        a_t.reshape(n_heads, head_w, bs).transpose(2, 0, 1).reshape(1, bs, 1, n_heads, head_w))
    return (y_p, y_s,
            token_major(fkt, FOX_HEADS, FOX_HD), token_major(fvt, FOX_HEADS, FOX_HD),
            lft.transpose(0, 2, 1)[None], stack(st_p, (bp, RET_HEADS, RET_DK, RET_DV)),
            stack(mk_p, (bp, n_mem, XA_HEADS, XA_HD)), stack(mv_p, (bp, n_mem, XA_HEADS, XA_HD)),
            sample_major(fkt_s, FOX_HEADS, FOX_HD), sample_major(fvt_s, FOX_HEADS, FOX_HD),
            lft_s[0].T.reshape(1, bs, 1, FOX_HEADS), stack(st_s, (bs, RET_HEADS, RET_DK, RET_DV)))
```

```python
import functools

import jax
import jax.numpy as jnp
from jax import lax
from jax.experimental import pallas as pl
from jax.experimental.pallas import tpu as pltpu

D_MODEL = 1024
FOX_HEADS = 8
FOX_HD = 64
RET_HEADS = 4
RET_DK = 64
RET_DV = 128
XA_HEADS = 4
XA_HD = 128
FOX_W = FOX_HEADS * FOX_HD
RET_QK_W = RET_HEADS * RET_DK
RET_V_W = RET_HEADS * RET_DV
XA_W = XA_HEADS * XA_HD
D_FF = 2816
RET_CHUNK = 128
ROPE_BASE = 10000.0
EPS = 1e-6
NEG_INF = -1e30

LANES = 128
VMEM_LIMIT = 56 * 1024 * 1024

F32 = jnp.float32
BF16 = jnp.bfloat16

_C_FQ, _C_FK, _C_FV = 0, 512, 1024
_C_RQK, _C_RV, _C_RG, _C_XQ, _C_GATES = 1536, 2048, 2560, 3072, 3584
_W_MAIN = 6656


def _params(sem):
    return pltpu.CompilerParams(dimension_semantics=sem, vmem_limit_bytes=VMEM_LIMIT)


def _dot(a, b):
    return jnp.dot(a, b, preferred_element_type=F32)


def _dot_nt(a, b):
    return lax.dot_general(a, b, (((1,), (1,)), ((), ())), preferred_element_type=F32)


def _dot_tn(a, b):
    return lax.dot_general(a, b, (((0,), (0,)), ((), ())), preferred_element_type=F32)


def _rms(x, g):
    return x * lax.rsqrt(jnp.mean(x * x, axis=-1, keepdims=True) + EPS) * g


def _log_sigmoid(x):
    return -(jnp.maximum(-x, 0.0) + jnp.log1p(jnp.exp(-jnp.abs(x))))


def _sigmoid(x):
    return 1.0 / (1.0 + jnp.exp(-x))


def _split3(x):
    hi = x.astype(BF16)
    r1 = x - hi.astype(F32)
    mid = r1.astype(BF16)
    lo = (r1 - mid.astype(F32)).astype(BF16)
    return hi, mid, lo


def _expand_heads(x, n_heads, head_w):
    width = n_heads * head_w
    lane = lax.broadcasted_iota(jnp.int32, (1, width), 1)
    out = jnp.zeros((1, width), F32)
    for h in range(n_heads):
        sel = (lane >= h * head_w) & (lane < (h + 1) * head_w)
        out = jnp.where(sel, x[:, h:h + 1], out)
    return out


def _inproj_kernel(x_ref, g_ref, wm_ref, wkvt_ref, wflt_ref, bft_ref, cos_ref, s1_ref, s2_ref,
                   gates_ref, fq_ref, fkt_ref, fvt_ref, fktb_ref, fkb_ref, fvb_ref, lft_ref, ct_ref,
                   rq_ref, rk_ref, rv_ref, rg_ref, xq_ref, carry_ref, *, tm):
    j = pl.program_id(1)
    hn = _rms(x_ref[0], g_ref[...]).astype(BF16)

    def proj(c0, width):
        return _dot(hn, wm_ref[:, c0:c0 + width])

    fq_ref[0] = (proj(_C_FQ, FOX_W) * (FOX_HD ** -0.5)).astype(BF16)
    fkb_ref[0] = proj(_C_FK, FOX_W).astype(BF16)
    fvb_ref[0] = proj(_C_FV, FOX_W).astype(BF16)
    fkt = _dot_nt(wkvt_ref[:FOX_W, :], hn)
    fkt_ref[0] = fkt
    fktb_ref[0] = fkt.astype(BF16)
    fvt_ref[0] = _dot_nt(wkvt_ref[FOX_W:, :], hn)

    rqk = proj(_C_RQK, 2 * RET_QK_W)
    cos, s1, s2 = cos_ref[...], s1_ref[...], s2_ref[...]
    for ref, off in ((rq_ref, 0), (rk_ref, RET_QK_W)):
        r = rqk[:, off:off + RET_QK_W]
        ref[0] = (r * cos + pltpu.roll(r, RET_QK_W - RET_DK // 2, 1) * s1
                  + pltpu.roll(r, RET_DK // 2, 1) * s2)
    rv_ref[0] = proj(_C_RV, RET_V_W)
    rg_ref[0] = proj(_C_RG, RET_V_W)
    xq_ref[0] = proj(_C_XQ, XA_W)
    for c in range(0, 3 * D_MODEL, 512):
        gates_ref[0, :, c:c + 512] = proj(_C_GATES + c, 512)

    lft = _log_sigmoid(_dot_nt(wflt_ref[...], hn) + bft_ref[...])
    lft_ref[0] = lft

    @pl.when(j == 0)
    def _():
        carry_ref[...] = jnp.zeros_like(carry_ref)

    row = lax.broadcasted_iota(jnp.int32, (tm, tm), 0)
    col = lax.broadcasted_iota(jnp.int32, (tm, tm), 1)
    tri = (row <= col).astype(BF16)
    hi, mid, lo = _split3(lft)
    ct = _dot(hi, tri) + _dot(mid, tri) + _dot(lo, tri) + carry_ref[:, 0:1]
    ct_ref[0] = ct
    carry_ref[...] = jnp.broadcast_to(ct[:, tm - 1:tm], carry_ref.shape)


def _inproj(x, g_attn, w_main, w_kvt, w_flt, b_f, cos, s1, s2, tm):
    b, s, _ = x.shape
    grid = (b, s // tm)
    tok = lambda w: pl.BlockSpec((1, tm, w), lambda i, j: (i, j, 0))
    feat = lambda h: pl.BlockSpec((1, h, tm), lambda i, j: (i, 0, j))
    const = lambda shape: pl.BlockSpec(shape, lambda i, j: (0,) * len(shape))
    tab = pl.BlockSpec((tm, RET_QK_W), lambda i, j: (j, 0))
    outs = [
        (tok, 3 * D_MODEL, F32), (tok, FOX_W, BF16),
        (feat, FOX_W, F32), (feat, FOX_W, F32), (feat, FOX_W, BF16),
        (tok, FOX_W, BF16), (tok, FOX_W, BF16),
        (feat, FOX_HEADS, F32), (feat, FOX_HEADS, F32),
        (tok, RET_QK_W, F32), (tok, RET_QK_W, F32), (tok, RET_V_W, F32), (tok, RET_V_W, F32), (tok, XA_W, F32),
    ]
    shape = lambda kind, w: (b, s, w) if kind is tok else (b, w, s)
    return pl.pallas_call(
        functools.partial(_inproj_kernel, tm=tm),
        grid=grid,
        in_specs=[tok(D_MODEL), const((1, D_MODEL)), const((D_MODEL, _W_MAIN)), const((2 * FOX_W, D_MODEL)),
                  const((FOX_HEADS, D_MODEL)), const((FOX_HEADS, 1)), tab, tab, tab],
        out_specs=[kind(w) for kind, w, _ in outs],
        out_shape=[jax.ShapeDtypeStruct(shape(kind, w), dt) for kind, w, dt in outs],
        scratch_shapes=[pltpu.VMEM((FOX_HEADS, LANES), F32)],
        compiler_params=_params(("arbitrary", "arbitrary")),
        name="inproj",
    )(x, g_attn.reshape(1, D_MODEL), w_main, w_kvt, w_flt, b_f.reshape(FOX_HEADS, 1), cos, s1, s2)


def _lane_blocks(x, op):
    out = x[:, :LANES]
    for c in range(LANES, x.shape[1], LANES):
        out = op(out, x[:, c:c + LANES])
    return out


_LOG2E = 1.4426950408889634


def _fox_prompt_kernel(q_ref, kt_ref, v_ref, ct_ref, o_ref, s_ref, mx_ref, l_ref, acc_ref, *, t):
    i = pl.program_id(2)
    lane = lax.broadcasted_iota(jnp.int32, (t, LANES), 1)
    row = lax.broadcasted_iota(jnp.int32, (t, t), 0)
    col = lax.broadcasted_iota(jnp.int32, (t, t), 1)
    chains = [(hh, half) for hh in range(2) for half in range(2)]
    qm = []
    for hh, half in chains:
        q = q_ref[0, half * t:(half + 1) * t, :]
        head_lanes = (lane >= hh * FOX_HD) & (lane < (hh + 1) * FOX_HD)
        qm.append(jnp.where(head_lanes, q, jnp.zeros_like(q)))

    def score_tile(c, j, diag):
        hh = chains[c][0]
        off = pl.multiple_of(j * t, t)
        s = (_dot(qm[c], kt_ref[0, :, pl.ds(off, t)]) - ct_ref[0, 0, hh:hh + 1, pl.ds(off, t)]) * _LOG2E
        if diag:
            s = jnp.where(col <= row, s, NEG_INF)
        s_ref[c, j] = s
        mx_ref[c] = jnp.maximum(mx_ref[c], _lane_blocks(s, jnp.maximum))

    tail = {0: ((0, True),), 1: ((0, False), (1, True))}

    mx_ref[...] = jnp.full_like(mx_ref, NEG_INF)

    def pass1(j, carry):
        for c in range(4):
            score_tile(c, j, False)
        return carry

    lax.fori_loop(0, 2 * i, pass1, 0)
    for c, (_, half) in enumerate(chains):
        for dj, diag in tail[half]:
            score_tile(c, 2 * i + dj, diag)

    m = [jnp.max(mx_ref[c], axis=1, keepdims=True) for c in range(4)]
    l_ref[...] = jnp.zeros_like(l_ref)
    acc_ref[...] = jnp.zeros_like(acc_ref)

    def value_tile(c, j):
        off = pl.multiple_of(j * t, t)
        p = jnp.exp2(s_ref[c, j] - m[c])
        l_ref[c] += _lane_blocks(p, jnp.add)
        acc_ref[c] += _dot(p.astype(BF16), v_ref[0, pl.ds(off, t), :])

    def pass2(j, carry):
        for c in range(4):
            value_tile(c, j)
        return carry

    lax.fori_loop(0, 2 * i, pass2, 0)
    for c, (_, half) in enumerate(chains):
        for dj, _ in tail[half]:
            value_tile(c, 2 * i + dj)

    out = [acc_ref[c] / jnp.sum(l_ref[c], axis=1, keepdims=True) for c in range(4)]
    for half in range(2):
        o_ref[0, half * t:(half + 1) * t, :] = jnp.where(lane < FOX_HD, out[half], out[2 + half])


def _fox_prompt(fqb, fktb, fvb, ct, t):
    b, s, _ = fqb.shape
    pairs = FOX_HEADS // 2
    ct4 = ct.reshape(b, pairs, 2, s)
    return pl.pallas_call(
        functools.partial(_fox_prompt_kernel, t=t),
        grid=(b, pairs, s // (2 * t)),
        in_specs=[pl.BlockSpec((1, 2 * t, LANES), lambda bi, pr, i: (bi, i, pr)),
                  pl.BlockSpec((1, LANES, s), lambda bi, pr, i: (bi, pr, 0)),
                  pl.BlockSpec((1, s, LANES), lambda bi, pr, i: (bi, 0, pr)),
                  pl.BlockSpec((1, 1, 2, s), lambda bi, pr, i: (bi, pr, 0, 0))],
        out_specs=pl.BlockSpec((1, 2 * t, LANES), lambda bi, pr, i: (bi, i, pr)),
        out_shape=jax.ShapeDtypeStruct((b, s, FOX_W), F32),
        scratch_shapes=[pltpu.VMEM((4, s // t, t, t), F32), pltpu.VMEM((4, t, LANES), F32),
                        pltpu.VMEM((4, t, LANES), F32), pltpu.VMEM((4, t, LANES), F32)],
        compiler_params=_params(("arbitrary", "arbitrary", "arbitrary")),
        name="fox_prompt",
    )(fqb, fktb, fvb, ct4)


def _ret_prompt_kernel(q_ref, k_ref, v_ref, rg_ref, gr_ref, dmask_ref, qdec_ref, kdec_ref, cdec_ref,
                       o_ref, st_ref):
    n = pl.program_id(1)

    @pl.when(n == 0)
    def _():
        st_ref[...] = jnp.zeros_like(st_ref)

    c = RET_CHUNK
    lane = lax.broadcasted_iota(jnp.int32, (c, LANES), 1)
    for bi in range(q_ref.shape[0]):
        for pr in range(RET_HEADS // 2):
            q = q_ref[bi, :, pr * LANES:(pr + 1) * LANES]
            k = k_ref[bi, :, pr * LANES:(pr + 1) * LANES] * (RET_DK ** -0.5)
            state = st_ref[bi, pr]
            qd = (q * qdec_ref[pr]).astype(BF16)
            kd = k * kdec_ref[pr]
            kb = k.astype(BF16)
            new_state = cdec_ref[pr] * state
            state_b = state.astype(BF16)
            for hh in range(2):
                vs = slice((2 * pr + hh) * RET_DV, (2 * pr + hh + 1) * RET_DV)
                head_lanes = (lane >= hh * RET_DK) & (lane < (hh + 1) * RET_DK)
                vb = v_ref[bi, :, vs].astype(BF16)
                qm = jnp.where(head_lanes, q, 0.0).astype(BF16)
                inner = _dot_nt(qm, kb) * dmask_ref[pr, hh]
                qdm = jnp.where(head_lanes, qd, jnp.zeros_like(qd))
                o = _dot(inner.astype(BF16), vb) + _dot(qdm, state_b)
                kdm = jnp.where(head_lanes, kd, 0.0).astype(BF16)
                new_state = new_state + _dot_tn(kdm, vb)
                mu = jnp.mean(o, axis=-1, keepdims=True)
                var = jnp.mean(jnp.square(o - mu), axis=-1, keepdims=True)
                y = (o - mu) * lax.rsqrt(var + EPS) * gr_ref[:, vs]
                rg = rg_ref[bi, :, vs]
                o_ref[bi, :, vs] = rg * _sigmoid(rg) * y
            st_ref[bi, pr] = new_state


def _ret_tables(length):
    h = RET_HEADS
    log_g = jnp.log(1.0 - 2.0 ** (-5.0 - jnp.arange(h, dtype=F32)))
    i = jnp.arange(length, dtype=F32)
    diff = i[:, None] - i[None, :]
    dmask = jnp.where(diff[None] >= 0, jnp.exp(jnp.maximum(diff, 0.0)[None] * log_g[:, None, None]), 0.0)
    q_dec = jnp.exp((i + 1.0)[:, None] * log_g[None, :])
    k_dec = jnp.exp((length - 1.0 - i)[:, None] * log_g[None, :])
    chunk_dec = jnp.exp(length * log_g)
    return dmask, q_dec, k_dec, chunk_dec


def _ret_prompt(rq, rk, rv, rg, g_ret):
    b, s, _ = rq.shape
    c = RET_CHUNK
    pairs = RET_HEADS // 2
    dmask, q_dec, k_dec, chunk_dec = _ret_tables(c)
    dmask = dmask.reshape(pairs, 2, c, c)
    qdec = jnp.repeat(q_dec, RET_DK, axis=1).reshape(c, pairs, LANES).transpose(1, 0, 2)
    kdec = jnp.repeat(k_dec, RET_DK, axis=1).reshape(c, pairs, LANES).transpose(1, 0, 2)
    cdec = jnp.repeat(chunk_dec, RET_DK).reshape(pairs, LANES, 1)
    bb = _pick_tile(b, 8)
    tok = lambda w: pl.BlockSpec((bb, c, w), lambda bi, n: (bi, n, 0))
    const = lambda shape: pl.BlockSpec(shape, lambda bi, n: (0,) * len(shape))
    o, st = pl.pallas_call(
        _ret_prompt_kernel,
        grid=(b // bb, s // c),
        in_specs=[tok(RET_QK_W), tok(RET_QK_W), tok(RET_V_W), tok(RET_V_W), const((1, RET_V_W)),
                  const((pairs, 2, c, c)), const((pairs, c, LANES)), const((pairs, c, LANES)),
                  const((pairs, LANES, 1))],
        out_specs=[tok(RET_V_W),
                   pl.BlockSpec((bb, pairs, 2 * RET_DK, RET_DV), lambda bi, n: (bi, 0, 0, 0))],
        out_shape=[jax.ShapeDtypeStruct((b, s, RET_V_W), F32),
                   jax.ShapeDtypeStruct((b, pairs, 2 * RET_DK, RET_DV), F32)],
        compiler_params=_params(("arbitrary", "arbitrary")),
        name="ret_prompt",
    )(rq, rk, rv, rg, g_ret.reshape(1, RET_V_W), dmask, qdec, kdec, cdec)
    return o, st.reshape(b, RET_HEADS, RET_DK, RET_DV)


def _ret_sample_kernel(q_ref, k_ref, v_ref, rg_ref, gr_ref, qdec_ref, cdec_ref, s0_ref, o_ref, s1_ref, *, bt):
    rows = RET_HEADS * RET_DK
    q = q_ref[...]
    k = k_ref[...] * (RET_DK ** -0.5)
    qg = q * qdec_ref[...]
    qk = q * k
    for h in range(RET_HEADS):
        v = v_ref[:, h * RET_DV:(h + 1) * RET_DV]
        cd = cdec_ref[:, h * RET_DV:(h + 1) * RET_DV]
        inner = jnp.sum(qk[:, h * RET_DK:(h + 1) * RET_DK], axis=1, keepdims=True)
        o = inner * v
        for d in range(RET_DK):
            r = h * RET_DK + d
            srow = s0_ref[pl.ds(r, bt, stride=rows), :]
            o = o + qg[:, r:r + 1] * srow
            s1_ref[pl.ds(r, bt, stride=rows), :] = cd * srow + k[:, r:r + 1] * v
        mu = jnp.mean(o, axis=-1, keepdims=True)
        var = jnp.mean(jnp.square(o - mu), axis=-1, keepdims=True)
        y = (o - mu) * lax.rsqrt(var + EPS) * gr_ref[:, h * RET_DV:(h + 1) * RET_DV]
        rg = rg_ref[:, h * RET_DV:(h + 1) * RET_DV]
        o_ref[:, h * RET_DV:(h + 1) * RET_DV] = rg * _sigmoid(rg) * y


def _ret_sample(rq, rk, rv, rg, g_ret, s0, bt):
    t = rq.shape[0]
    rows = RET_HEADS * RET_DK
    _, q_dec, _, chunk_dec = _ret_tables(1)
    qdec = jnp.repeat(q_dec, RET_DK, axis=1).reshape(1, RET_QK_W)
    cdec = jnp.repeat(chunk_dec, RET_DV).reshape(1, RET_V_W)
    tok = lambda w: pl.BlockSpec((bt, w), lambda i: (i, 0))
    const = lambda w: pl.BlockSpec((1, w), lambda i: (0, 0))
    st = pl.BlockSpec((bt * rows, RET_DV), lambda i: (i, 0))
    o, s1 = pl.pallas_call(
        functools.partial(_ret_sample_kernel, bt=bt),
        grid=(t // bt,),
        in_specs=[tok(RET_QK_W), tok(RET_QK_W), tok(RET_V_W), tok(RET_V_W), const(RET_V_W),
                  const(RET_QK_W), const(RET_V_W), st],
        out_specs=[tok(RET_V_W), st],
        out_shape=[jax.ShapeDtypeStruct((t, RET_V_W), F32), jax.ShapeDtypeStruct((t * rows, RET_DV), F32)],
        compiler_params=_params(("arbitrary",)),
        name="ret_sample",
    )(rq, rk, rv, rg, g_ret.reshape(1, RET_V_W), qdec, cdec, s0.reshape(t * rows, RET_DV))
    return o, s1.reshape(t, RET_HEADS, RET_DK, RET_DV)


def _xattn_prompt_kernel(q_ref, mk_ref, mv_ref, o_ref):
    scale = XA_HD ** -0.5
    for h in range(XA_HEADS):
        sl = slice(h * XA_HD, (h + 1) * XA_HD)
        s = _dot_nt(q_ref[0, :, sl].astype(BF16), mk_ref[0, :, sl].astype(BF16)) * scale
        p = jnp.exp(s - jnp.max(s, axis=1, keepdims=True))
        p = p / jnp.sum(p, axis=1, keepdims=True)
        o_ref[0, :, sl] = _dot(p.astype(BF16), mv_ref[0, :, sl].astype(BF16))


def _xattn_prompt(xq, mk, mv, t):
    b, s, _ = xq.shape
    m = mk.shape[1]
    return pl.pallas_call(
        _xattn_prompt_kernel,
        grid=(b, s // t),
        in_specs=[pl.BlockSpec((1, t, XA_W), lambda bi, i: (bi, i, 0)),
                  pl.BlockSpec((1, m, XA_W), lambda bi, i: (bi, 0, 0)),
                  pl.BlockSpec((1, m, XA_W), lambda bi, i: (bi, 0, 0))],
        out_specs=pl.BlockSpec((1, t, XA_W), lambda bi, i: (bi, i, 0)),
        out_shape=jax.ShapeDtypeStruct((b, s, XA_W), F32),
        compiler_params=_params(("arbitrary", "arbitrary")),
        name="xattn_prompt",
    )(xq, mk, mv)


def _head_query_block(q_row, n_cols, head_w):
    w = q_row.shape[1]
    r = lax.broadcasted_iota(jnp.int32, (w, w), 0)
    c = lax.broadcasted_iota(jnp.int32, (w, w), 1)
    diag = jnp.where(r == c, jnp.broadcast_to(q_row, (w, w)), 0.0)
    rr = lax.broadcasted_iota(jnp.int32, (w, n_cols), 0)
    cc = lax.broadcasted_iota(jnp.int32, (w, n_cols), 1)
    ones = ((rr >= cc * head_w) & (rr < (cc + 1) * head_w)).astype(F32)
    return _dot(diag, ones)


def _head_expand_matrix(n_rows, n_heads, head_w):
    width = n_heads * head_w
    rr = lax.broadcasted_iota(jnp.int32, (n_rows, width), 0)
    cc = lax.broadcasted_iota(jnp.int32, (n_rows, width), 1)
    return ((cc >= rr * head_w) & (cc < (rr + 1) * head_w)).astype(F32)


_HPAD = 8


def _xattn_sample_kernel(q_ref, mk_ref, mv_ref, o_ref):
    qblk = _head_query_block(q_ref[0], _HPAD, XA_HD)
    s = _dot(mk_ref[0], qblk) * (XA_HD ** -0.5)
    m = jnp.max(s, axis=0, keepdims=True)
    p = jnp.exp(s - m)
    l = jnp.sum(p, axis=0, keepdims=True)
    pe = _dot(p, _head_expand_matrix(_HPAD, XA_HEADS, XA_HD))
    acc = jnp.sum(pe * mv_ref[0], axis=0, keepdims=True)
    o_ref[0] = acc / _expand_heads(l, XA_HEADS, XA_HD)


def _xattn_sample(xq, mk, mv):
    t, m, _ = mk.shape
    o = pl.pallas_call(
        _xattn_sample_kernel,
        grid=(t,),
        in_specs=[pl.BlockSpec((1, 1, XA_W), lambda i: (i, 0, 0)),
                  pl.BlockSpec((1, m, XA_W), lambda i: (i, 0, 0)),
                  pl.BlockSpec((1, m, XA_W), lambda i: (i, 0, 0))],
        out_specs=pl.BlockSpec((1, 1, XA_W), lambda i: (i, 0, 0)),
        out_shape=jax.ShapeDtypeStruct((t, 1, XA_W), F32),
        compiler_params=_params(("arbitrary",)),
        name="xattn_sample",
    )(xq.reshape(t, 1, XA_W), mk, mv)
    return o.reshape(t, XA_W)


_DECODE_PAGES = 16
_DECODE_SLOTS = 4


def _suffix_sum_lanes(x):
    n = x.shape[1]
    lane = lax.broadcasted_iota(jnp.int32, x.shape, 1)
    sh = 1
    while sh < n:
        x = x + jnp.where(lane < n - sh, pltpu.roll(x, n - sh, 1), 0.0)
        sh *= 2
    return x


def _fox_sample_kernel(pt_ref, q_ref, cn_ref, kn_ref, vn_ref, ck_hbm, clf_hbm, cv_hbm, o_ref,
                       buf_ref, lfbuf_ref, sem, lfsem, qb_ref, part_ref, p_ref, s_ref, m_ref, sn_ref, cnb_ref,
                       carry_ref, acc_ref, l_ref, *, pps, n_steps, n_pages, n_tok, n_slots):
    hrow = lax.broadcasted_iota(jnp.int32, (FOX_HEADS, FOX_W), 0)
    hlane = lax.broadcasted_iota(jnp.int32, (FOX_HEADS, FOX_W), 1)
    head_mask = (hlane >= hrow * FOX_HD) & (hlane < (hrow + 1) * FOX_HD)

    def rows_to_col(row_vals):
        r8 = lax.broadcasted_iota(jnp.int32, (FOX_HEADS, FOX_HEADS), 0)
        c8 = lax.broadcasted_iota(jnp.int32, (FOX_HEADS, FOX_HEADS), 1)
        b = jnp.broadcast_to(row_vals, (FOX_HEADS, FOX_HEADS))
        return jnp.sum(jnp.where(r8 == c8, b, 0.0), axis=1, keepdims=True)

    def col_to_head_row(col_vals):
        b = jnp.broadcast_to(col_vals, (FOX_HEADS, FOX_W))
        return jnp.sum(jnp.where(head_mask, b, 0.0), axis=0, keepdims=True)

    def unrolled(n, body, init=0):
        return lax.fori_loop(0, n, body, init, unroll=True)

    def static_when(cond):
        def run(f):
            if cond:
                f()
        return run

    def page_copies(keys, b, j, slot, i):
        page = 0 if b is None else pt_ref[b, n_pages - 1 - (j * pps + i)]
        if keys:
            return (pltpu.make_async_copy(ck_hbm.at[page], buf_ref.at[slot, i], sem.at[slot]),
                    pltpu.make_async_copy(clf_hbm.at[page], lfbuf_ref.at[slot, i], lfsem.at[slot]))
        return (pltpu.make_async_copy(cv_hbm.at[page], buf_ref.at[slot, i], sem.at[slot]),)

    def start_chunk(keys, b, j, slot):
        def body(i, c):
            for cp in page_copies(keys, b, j, slot, i):
                cp.start()
            return c
        unrolled(pps, body)

    def wait_chunk(keys, slot):
        def body(i, c):
            for cp in page_copies(keys, None, 0, slot, i):
                cp.wait()
            return c
        unrolled(pps, body)

    def key_chunk(b, j, slot):
        @static_when(j == 0)
        def _():
            q = q_ref[pl.ds(b, 1), :]
            qb_ref[...] = jnp.transpose(jnp.broadcast_to(q, (LANES, FOX_W)))
            cnb_ref[...] = jnp.broadcast_to(rows_to_col(cn_ref[pl.ds(b, 1), :]), cnb_ref.shape)
            carry_ref[...] = jnp.zeros_like(carry_ref)
            m_ref[...] = jnp.full_like(m_ref, NEG_INF)

        def head(h, c):
            qbh = qb_ref[pl.ds(pl.multiple_of(h * FOX_HD, FOX_HD), FOX_HD), :]

            def page_part(i, c2):
                prod = buf_ref[slot, i, h] * qbh
                part_ref[i, h] = jnp.sum(prod.reshape(FOX_HD // 8, 8, LANES), axis=0)
                return c2
            return unrolled(pps, page_part, c)
        unrolled(FOX_HEADS, head)

        cnb = cnb_ref[...]

        def page_scores(i, mc):
            m, carry = mc
            lf = lfbuf_ref[slot, i]
            suf = _suffix_sum_lanes(lf)
            s = jnp.sum(part_ref[i], axis=1) + cnb + ((suf - lf) + carry)
            s_ref[j * pps + i] = s
            return jnp.maximum(m, s), carry + suf[:, 0:1]
        m, carry = unrolled(pps, page_scores, (m_ref[...], carry_ref[...]))
        carry_ref[...] = carry
        m_ref[...] = m

        @static_when(j == n_steps - 1)
        def _():
            q8 = jnp.where(head_mask, jnp.broadcast_to(q_ref[pl.ds(b, 1), :], (FOX_HEADS, FOX_W)), 0.0)
            sn = jnp.sum(q8 * kn_ref[pl.ds(b, 1), :], axis=1, keepdims=True)
            mf = jnp.maximum(jnp.max(m, axis=1, keepdims=True), sn)
            sn_ref[...] = jnp.broadcast_to(sn, sn_ref.shape)
            m_ref[...] = jnp.broadcast_to(mf, m_ref.shape)

    def value_chunk(b, j, slot):
        @static_when(j == 0)
        def _():
            acc_ref[...] = jnp.zeros_like(acc_ref)
            l_ref[...] = jnp.zeros_like(l_ref)

        m = m_ref[...]

        def page_probs(i, l):
            p = jnp.exp(s_ref[j * pps + i] - m)
            p_ref[i] = p
            return l + p
        l = unrolled(pps, page_probs, l_ref[...])
        l_ref[...] = l

        def head(h, c):
            def page_acc(i, a):
                return a + p_ref[i, pl.ds(h, 1), :] * buf_ref[slot, i, h]
            acc_ref[h] = unrolled(pps, page_acc, acc_ref[h])
            return c
        unrolled(FOX_HEADS, head)

        @static_when(j == n_steps - 1)
        def _():
            pn = jnp.exp(sn_ref[:, 0:1] - m[:, 0:1])
            lsum = jnp.sum(l, axis=1, keepdims=True) + pn
            acc_t = jnp.transpose(acc_ref[...].reshape(FOX_W, LANES))
            acc = jnp.sum(acc_t, axis=0, keepdims=True) + col_to_head_row(pn) * vn_ref[pl.ds(b, 1), :]
            o_ref[pl.ds(b, 1), :] = acc / col_to_head_row(lsum)

    n_chunks = 2 * n_steps
    ahead = n_slots - 1

    def start_ahead(b, r):
        slot = r % n_slots
        if r < n_chunks:
            start_chunk(r < n_steps, b, r % n_steps, slot)
        else:
            @pl.when(b + 1 < n_tok)
            def _():
                start_chunk(True, b + 1, r - n_chunks, slot)

    for r in range(ahead):
        start_chunk(True, 0, r, r % n_slots)

    def token(b, carry):
        for r in range(n_chunks):
            slot = r % n_slots
            wait_chunk(r < n_steps, slot)
            start_ahead(b, r + ahead)
            if r < n_steps:
                key_chunk(b, r, slot)
            else:
                value_chunk(b, r - n_steps, slot)
        return carry

    lax.fori_loop(0, n_tok, token, 0)


def _fox_sample(page_table, fq, logf, fk, fv, ckt, cvt, clft, pps):
    t, n_pages = page_table.shape
    page = ckt.shape[3]
    assert page == LANES
    n_steps = n_pages // pps
    n_slots = _DECODE_SLOTS if (2 * n_steps) % _DECODE_SLOTS == 0 and _DECODE_SLOTS - 1 <= n_steps else 2
    full = lambda w: pl.BlockSpec((t, w), lambda i, pt: (0, 0))
    hbm = pl.BlockSpec(memory_space=pl.ANY)
    hb = (FOX_HEADS, LANES)
    grid_spec = pltpu.PrefetchScalarGridSpec(
        num_scalar_prefetch=1,
        grid=(1,),
        in_specs=[full(FOX_W), full(FOX_HEADS), full(FOX_W), full(FOX_W), hbm, hbm, hbm],
        out_specs=full(FOX_W),
        scratch_shapes=[pltpu.VMEM((n_slots, pps, FOX_HEADS, FOX_HD, LANES), F32),
                        pltpu.VMEM((n_slots, pps) + hb, F32),
                        pltpu.SemaphoreType.DMA((n_slots,)), pltpu.SemaphoreType.DMA((n_slots,)),
                        pltpu.VMEM((FOX_W, LANES), F32),
                        pltpu.VMEM((pps, FOX_HEADS, 8, LANES), F32),
                        pltpu.VMEM((pps,) + hb, F32),
                        pltpu.VMEM((n_pages,) + hb, F32),
                        pltpu.VMEM(hb, F32), pltpu.VMEM(hb, F32), pltpu.VMEM(hb, F32), pltpu.VMEM(hb, F32),
                        pltpu.VMEM((FOX_HEADS, FOX_HD, LANES), F32), pltpu.VMEM(hb, F32)],
    )
    return pl.pallas_call(
        functools.partial(_fox_sample_kernel, pps=pps, n_steps=n_steps, n_pages=n_pages, n_tok=t,
                          n_slots=n_slots),
        grid_spec=grid_spec,
        out_shape=jax.ShapeDtypeStruct((t, FOX_W), F32),
        compiler_params=_params(("arbitrary",)),
        name="fox_sample",
    )(page_table, fq, logf, fk, fv, ckt, clft, cvt)


def _memkv_kernel(x_ref, g_ref, w_ref, mk_ref, mv_ref):
    hn = _rms(x_ref[...], g_ref[...]).astype(BF16)
    mk_ref[...] = _dot(hn, w_ref[:, :XA_W])
    mv_ref[...] = _dot(hn, w_ref[:, XA_W:])


def _memkv(mem, g_mem, w_kv, tm):
    rows = mem.shape[0]
    return pl.pallas_call(
        _memkv_kernel,
        grid=(rows // tm,),
        in_specs=[pl.BlockSpec((tm, D_MODEL), lambda i: (i, 0)),
                  pl.BlockSpec((1, D_MODEL), lambda i: (0, 0)),
                  pl.BlockSpec((D_MODEL, 2 * XA_W), lambda i: (0, 0))],
        out_specs=[pl.BlockSpec((tm, XA_W), lambda i: (i, 0))] * 2,
        out_shape=[jax.ShapeDtypeStruct((rows, XA_W), F32)] * 2,
        compiler_params=_params(("arbitrary",)),
        name="memkv",
    )(mem, g_mem.reshape(1, D_MODEL), w_kv)


_FF_CHUNK = 1408


def _post_kernel(x_ref, gates_ref, of_ref, or_ref, ox_ref, wf_ref, wr_ref, wx_ref, wo_ref, gffn_ref,
                 wgu_ref, wd_ref, gfin_ref, y_ref):
    mix = None
    for idx, (o_ref, w_ref) in enumerate(((of_ref, wf_ref), (or_ref, wr_ref), (ox_ref, wx_ref))):
        gate = _sigmoid(gates_ref[:, idx * D_MODEL:(idx + 1) * D_MODEL])
        term = gate * _dot(o_ref[...].astype(BF16), w_ref[...])
        mix = term if mix is None else mix + term
    x = x_ref[...] + _dot(mix.astype(BF16), wo_ref[...])
    hb = _rms(x, gffn_ref[...]).astype(BF16)
    ffn = jnp.zeros_like(x)
    for c in range(0, D_FF, _FF_CHUNK):
        u_gate = _dot(hb, wgu_ref[:, c:c + _FF_CHUNK])
        u_up = _dot(hb, wgu_ref[:, D_FF + c:D_FF + c + _FF_CHUNK])
        act = (u_gate * _sigmoid(u_gate) * u_up).astype(BF16)
        ffn = ffn + _dot(act, wd_ref[c:c + _FF_CHUNK, :])
    y_ref[...] = _rms(x + ffn, gfin_ref[...])


def _post(x, gates, o_fox, o_ret, o_xa, wf, wr, wx, wo, g_ffn, wgu, wd, g_final, tm):
    rows = x.shape[0]
    row = lambda w: pl.BlockSpec((tm, w), lambda i: (i, 0))
    const = lambda a, b: pl.BlockSpec((a, b), lambda i: (0, 0))
    return pl.pallas_call(
        _post_kernel,
        grid=(rows // tm,),
        in_specs=[row(D_MODEL), row(3 * D_MODEL), row(FOX_W), row(RET_V_W), row(XA_W),
                  const(FOX_W, D_MODEL), const(RET_V_W, D_MODEL), const(XA_W, D_MODEL),
                  const(D_MODEL, D_MODEL), const(1, D_MODEL), const(D_MODEL, 2 * D_FF),
                  const(D_FF, D_MODEL), const(1, D_MODEL)],
        out_specs=row(D_MODEL),
        out_shape=jax.ShapeDtypeStruct((rows, D_MODEL), F32),
        compiler_params=_params(("arbitrary",)),
        name="post",
    )(x, gates, o_fox, o_ret, o_xa, wf, wr, wx, wo, g_ffn.reshape(1, D_MODEL), wgu, wd,
      g_final.reshape(1, D_MODEL))


def _rotary_tables(pos):
    half = RET_DK // 2
    inv = ROPE_BASE ** (-jnp.arange(half, dtype=F32) / half)
    ang = pos.astype(F32)[:, None] * inv[None, :]
    cos, sin = jnp.cos(ang), jnp.sin(ang)
    zero = jnp.zeros_like(sin)
    tile = lambda a, b: jnp.tile(jnp.concatenate([a, b], axis=1), (1, RET_HEADS))
    return tile(cos, cos), tile(-sin, zero), tile(zero, sin)


def _pick_tile(n, pref):
    t = min(n, pref)
    while n % t:
        t //= 2
    return t


def kernel(x_prompt, x_sample, mem_prompt, cache_fox_k, cache_fox_v, cache_fox_logf, state_ret, cache_mem_k, cache_mem_v, page_table, g_attn, w_in, b_f, g_ret, w_br_fox, w_br_ret, w_br_xa, w_o, g_ffn, w_gu, w_down, g_mem, w_mem_kv, g_final):
    bp, sp, _ = x_prompt.shape
    bs, ts, _ = x_sample.shape
    depth = w_in.shape[0]
    assert depth == 1 and ts == 1
    n_pages, page = page_table.shape[1], cache_fox_k.shape[2]
    past = n_pages * page
    n_mem = mem_prompt.shape[1]
    l = 0

    flog0 = 3 * FOX_W
    w_l = w_in[l]
    w_main = jnp.concatenate([w_l[:, :flog0], w_l[:, flog0 + FOX_HEADS:]], axis=1).astype(BF16)
    w_kvt = w_l[:, FOX_W:flog0].T.astype(BF16)
    w_flt = w_l[:, flog0:flog0 + FOX_HEADS].T.astype(BF16)
    wf, wr, wx = w_br_fox[l].astype(BF16), w_br_ret[l].astype(BF16), w_br_xa[l].astype(BF16)
    wo, wgu, wd = w_o[l].astype(BF16), w_gu[l].astype(BF16), w_down[l].astype(BF16)
    w_kv = w_mem_kv[l].astype(BF16)

    def token_major(a_t, n_heads, head_w):
        b, _, s = a_t.shape
        return a_t.reshape(b, n_heads, head_w, s).transpose(0, 3, 1, 2)[None]

    mk_p, mv_p = _memkv(mem_prompt.reshape(bp * n_mem, D_MODEL), g_mem[l], w_kv, _pick_tile(bp * n_mem, 256))
    mk_p = mk_p.reshape(bp, n_mem, XA_W)
    mv_p = mv_p.reshape(bp, n_mem, XA_W)
    tm = _pick_tile(sp, 256)
    cos, s1, s2 = _rotary_tables(jnp.arange(sp))
    (gates, fqb, fkt, fvt, fktb, _, fvb, lft, ct, rq, rk, rv, rg, xq) = _inproj(
        x_prompt, g_attn[l], w_main, w_kvt, w_flt, b_f[l], cos, s1, s2, tm)
    o_fox = _fox_prompt(fqb, fktb, fvb, ct, _pick_tile(sp // 2, 512))
    o_ret, st_p = _ret_prompt(rq, rk, rv, rg, g_ret[l])
    o_xa = _xattn_prompt(xq, mk_p, mv_p, _pick_tile(sp, 256))
    rows = bp * sp
    y_p = _post(x_prompt.reshape(rows, D_MODEL), gates.reshape(rows, 3 * D_MODEL), o_fox.reshape(rows, FOX_W),
                o_ret.reshape(rows, RET_V_W), o_xa.reshape(rows, XA_W), wf, wr, wx, wo, g_ffn[l], wgu, wd,
                g_final, _pick_tile(rows, 256)).reshape(bp, sp, D_MODEL)

    cos_s, s1_s, s2_s = _rotary_tables(jnp.full((bs,), past, jnp.int32))
    (gates_s, fqb_s, fkt_s, fvt_s, _, fkb_s, fvb_s, lft_s, _, rq_s, rk_s, rv_s, rg_s, xq_s) = _inproj(
        x_sample.reshape(1, bs, D_MODEL), g_attn[l], w_main, w_kvt, w_flt, b_f[l], cos_s, s1_s, s2_s, bs)
    two = lambda a: a.reshape(bs, a.shape[-1])
    ckt = cache_fox_k[l].transpose(0, 2, 3, 1)
    cvt = cache_fox_v[l].transpose(0, 2, 3, 1)
    clft = cache_fox_logf[l].transpose(0, 2, 1)
    o_fox_s = _fox_sample(page_table, two(fqb_s).astype(F32), lft_s[0].T, two(fkb_s).astype(F32),
                          two(fvb_s).astype(F32), ckt, cvt, clft, _pick_tile(n_pages, _DECODE_PAGES))
    o_ret_s, st_s = _ret_sample(two(rq_s), two(rk_s), two(rv_s), two(rg_s), g_ret[l], state_ret[l],
                                _pick_tile(bs, 32))
    o_xa_s = _xattn_sample(two(xq_s), cache_mem_k[l].reshape(bs, n_mem, XA_W),
                           cache_mem_v[l].reshape(bs, n_mem, XA_W))
    y_s = _post(x_sample.reshape(bs, D_MODEL), two(gates_s), o_fox_s, o_ret_s, o_xa_s, wf, wr, wx, wo,
                g_ffn[l], wgu, wd, g_final, bs).reshape(bs, 1, D_MODEL)

    stack = lambda a, shape: a.reshape((1,) + shape)
    sample_major = lambda a_t, n_heads, head_w: (
        a_t.reshape(n_heads, head_w, bs).transpose(2, 0, 1).reshape(1, bs, 1, n_heads, head_w))
    return (y_p, y_s,
            token_major(fkt, FOX_HEADS, FOX_HD), token_major(fvt, FOX_HEADS, FOX_HD),
            lft.transpose(0, 2, 1)[None], stack(st_p, (bp, RET_HEADS, RET_DK, RET_DV)),
            stack(mk_p, (bp, n_mem, XA_HEADS, XA_HD)), stack(mv_p, (bp, n_mem, XA_HEADS, XA_HD)),
            sample_major(fkt_s, FOX_HEADS, FOX_HD), sample_major(fvt_s, FOX_HEADS, FOX_HD),
            lft_s[0].T.reshape(1, bs, 1, FOX_HEADS), stack(st_s, (bs, RET_HEADS, RET_DK, RET_DV)))
```

```python
import functools

import jax
import jax.numpy as jnp
from jax import lax
from jax.experimental import pallas as pl
from jax.experimental.pallas import tpu as pltpu

D_MODEL = 1024
FOX_HEADS = 8
FOX_HD = 64
RET_HEADS = 4
RET_DK = 64
RET_DV = 128
XA_HEADS = 4
XA_HD = 128
FOX_W = FOX_HEADS * FOX_HD
RET_QK_W = RET_HEADS * RET_DK
RET_V_W = RET_HEADS * RET_DV
XA_W = XA_HEADS * XA_HD
D_FF = 2816
RET_CHUNK = 128
ROPE_BASE = 10000.0
EPS = 1e-6
NEG_INF = -1e30

LANES = 128
VMEM_LIMIT = 56 * 1024 * 1024

F32 = jnp.float32
BF16 = jnp.bfloat16

_C_FQ, _C_FK, _C_FV = 0, 512, 1024
_C_RQK, _C_RV, _C_RG, _C_XQ, _C_GATES = 1536, 2048, 2560, 3072, 3584
_W_MAIN = 6656


def _params(sem):
    return pltpu.CompilerParams(dimension_semantics=sem, vmem_limit_bytes=VMEM_LIMIT)


def _dot(a, b):
    return jnp.dot(a, b, preferred_element_type=F32)


def _dot_nt(a, b):
    return lax.dot_general(a, b, (((1,), (1,)), ((), ())), preferred_element_type=F32)


def _dot_tn(a, b):
    return lax.dot_general(a, b, (((0,), (0,)), ((), ())), preferred_element_type=F32)


def _rms(x, g):
    return x * lax.rsqrt(jnp.mean(x * x, axis=-1, keepdims=True) + EPS) * g


def _log_sigmoid(x):
    return -(jnp.maximum(-x, 0.0) + jnp.log1p(jnp.exp(-jnp.abs(x))))


def _sigmoid(x):
    return 1.0 / (1.0 + jnp.exp(-x))


def _split3(x):
    hi = x.astype(BF16)
    r1 = x - hi.astype(F32)
    mid = r1.astype(BF16)
    lo = (r1 - mid.astype(F32)).astype(BF16)
    return hi, mid, lo


def _inproj_kernel(x_ref, g_ref, wm_ref, wkvt_ref, wflt_ref, bft_ref, cos_ref, s1_ref, s2_ref,
                   gates_ref, fq_ref, fkt_ref, fvt_ref, fktb_ref, fvb_ref, lft_ref, ct_ref,
                   rq_ref, rk_ref, rv_ref, rg_ref, xq_ref, *tail, tm):
    carry_ref = tail[-1]
    j = pl.program_id(1)
    hn = _rms(x_ref[0], g_ref[...]).astype(BF16)

    def proj(c0, width):
        return _dot(hn, wm_ref[:, c0:c0 + width])

    fq_ref[0] = (proj(_C_FQ, FOX_W) * (FOX_HD ** -0.5)).astype(BF16)
    if len(tail) == 2:
        tail[0][0] = proj(_C_FK, FOX_W).astype(BF16)
    fvb_ref[0] = proj(_C_FV, FOX_W).astype(BF16)
    fkt = _dot_nt(wkvt_ref[:FOX_W, :], hn)
    fkt_ref[0] = fkt
    fktb_ref[0] = fkt.astype(BF16)
    fvt_ref[0] = _dot_nt(wkvt_ref[FOX_W:, :], hn)

    rqk = proj(_C_RQK, 2 * RET_QK_W)
    cos, s1, s2 = cos_ref[...], s1_ref[...], s2_ref[...]
    for ref, off in ((rq_ref, 0), (rk_ref, RET_QK_W)):
        r = rqk[:, off:off + RET_QK_W]
        ref[0] = (r * cos + pltpu.roll(r, RET_QK_W - RET_DK // 2, 1) * s1
                  + pltpu.roll(r, RET_DK // 2, 1) * s2)
    rv_ref[0] = proj(_C_RV, RET_V_W)
    rg_ref[0] = proj(_C_RG, RET_V_W)
    xq_ref[0] = proj(_C_XQ, XA_W)
    for c in range(0, 3 * D_MODEL, 512):
        gates_ref[0, :, c:c + 512] = proj(_C_GATES + c, 512)

    lft = _log_sigmoid(_dot_nt(wflt_ref[...], hn) + bft_ref[...])
    lft_ref[0] = lft

    @pl.when(j == 0)
    def _():
        carry_ref[...] = jnp.zeros_like(carry_ref)

    row = lax.broadcasted_iota(jnp.int32, (tm, tm), 0)
    col = lax.broadcasted_iota(jnp.int32, (tm, tm), 1)
    tri = (row <= col).astype(BF16)
    hi, mid, lo = _split3(lft)
    ct = _dot(hi, tri) + _dot(mid, tri) + _dot(lo, tri) + carry_ref[:, 0:1]
    ct_ref[0] = ct
    carry_ref[...] = jnp.broadcast_to(ct[:, tm - 1:tm], carry_ref.shape)


def _inproj(x, g_attn, w_main, w_kvt, w_flt, b_f, cos, s1, s2, tm, token_major_k):
    b, s, _ = x.shape
    grid = (b, s // tm)
    tok = lambda w: pl.BlockSpec((1, tm, w), lambda i, j: (i, j, 0))
    feat = lambda h: pl.BlockSpec((1, h, tm), lambda i, j: (i, 0, j))
    const = lambda shape: pl.BlockSpec(shape, lambda i, j: (0,) * len(shape))
    tab = pl.BlockSpec((tm, RET_QK_W), lambda i, j: (j, 0))
    outs = [
        (tok, 3 * D_MODEL, F32), (tok, FOX_W, BF16),
        (feat, FOX_W, F32), (feat, FOX_W, F32), (feat, FOX_W, BF16),
        (tok, FOX_W, BF16),
        (feat, FOX_HEADS, F32), (feat, FOX_HEADS, F32),
        (tok, RET_QK_W, F32), (tok, RET_QK_W, F32), (tok, RET_V_W, F32), (tok, RET_V_W, F32), (tok, XA_W, F32),
    ] + ([(tok, FOX_W, BF16)] if token_major_k else [])
    shape = lambda kind, w: (b, s, w) if kind is tok else (b, w, s)
    return pl.pallas_call(
        functools.partial(_inproj_kernel, tm=tm),
        grid=grid,
        in_specs=[tok(D_MODEL), const((1, D_MODEL)), const((D_MODEL, _W_MAIN)), const((2 * FOX_W, D_MODEL)),
                  const((FOX_HEADS, D_MODEL)), const((FOX_HEADS, 1)), tab, tab, tab],
        out_specs=[kind(w) for kind, w, _ in outs],
        out_shape=[jax.ShapeDtypeStruct(shape(kind, w), dt) for kind, w, dt in outs],
        scratch_shapes=[pltpu.VMEM((FOX_HEADS, LANES), F32)],
        compiler_params=_params(("arbitrary", "arbitrary")),
        name="inproj",
    )(x, g_attn.reshape(1, D_MODEL), w_main, w_kvt, w_flt, b_f.reshape(FOX_HEADS, 1), cos, s1, s2)


def _lane_blocks(x, op):
    out = x[:, :LANES]
    for c in range(LANES, x.shape[1], LANES):
        out = op(out, x[:, c:c + LANES])
    return out


_LOG2E = 1.4426950408889634


def _fox_prompt_kernel(q_ref, kt_ref, v_ref, ct_ref, o_ref, s_ref, mx_ref, l_ref, acc_ref, *, t):
    i = pl.program_id(2)
    lane = lax.broadcasted_iota(jnp.int32, (t, LANES), 1)
    row = lax.broadcasted_iota(jnp.int32, (t, t), 0)
    col = lax.broadcasted_iota(jnp.int32, (t, t), 1)
    chains = [(hh, half) for hh in range(2) for half in range(2)]
    qm = []
    for hh, half in chains:
        q = q_ref[0, half * t:(half + 1) * t, :]
        head_lanes = (lane >= hh * FOX_HD) & (lane < (hh + 1) * FOX_HD)
        qm.append(jnp.where(head_lanes, q, jnp.zeros_like(q)))

    def score_tile(c, j, diag):
        hh = chains[c][0]
        off = pl.multiple_of(j * t, t)
        s = (_dot(qm[c], kt_ref[0, :, pl.ds(off, t)]) - ct_ref[0, 0, hh:hh + 1, pl.ds(off, t)]) * _LOG2E
        if diag:
            s = jnp.where(col <= row, s, NEG_INF)
        s_ref[c, j] = s
        mx_ref[c] = jnp.maximum(mx_ref[c], _lane_blocks(s, jnp.maximum))

    tail = {0: ((0, True),), 1: ((0, False), (1, True))}

    mx_ref[...] = jnp.full_like(mx_ref, NEG_INF)

    def pass1(j, carry):
        for c in range(4):
            score_tile(c, j, False)
        return carry

    lax.fori_loop(0, 2 * i, pass1, 0)
    for c, (_, half) in enumerate(chains):
        for dj, diag in tail[half]:
            score_tile(c, 2 * i + dj, diag)

    m = [jnp.max(mx_ref[c], axis=1, keepdims=True) for c in range(4)]
    l_ref[...] = jnp.zeros_like(l_ref)
    acc_ref[...] = jnp.zeros_like(acc_ref)

    def value_tile(c, j):
        off = pl.multiple_of(j * t, t)
        p = jnp.exp2(s_ref[c, j] - m[c])
        l_ref[c] += _lane_blocks(p, jnp.add)
        acc_ref[c] += _dot(p.astype(BF16), v_ref[0, pl.ds(off, t), :])

    def pass2(j, carry):
        for c in range(4):
            value_tile(c, j)
        return carry

    lax.fori_loop(0, 2 * i, pass2, 0)
    for c, (_, half) in enumerate(chains):
        for dj, _ in tail[half]:
            value_tile(c, 2 * i + dj)

    out = [acc_ref[c] / jnp.sum(l_ref[c], axis=1, keepdims=True) for c in range(4)]
    for half in range(2):
        o_ref[0, half * t:(half + 1) * t, :] = jnp.where(lane < FOX_HD, out[half], out[2 + half])


def _fox_prompt(fqb, fktb, fvb, ct, t):
    b, s, _ = fqb.shape
    pairs = FOX_HEADS // 2
    ct4 = ct.reshape(b, pairs, 2, s)
    return pl.pallas_call(
        functools.partial(_fox_prompt_kernel, t=t),
        grid=(b, pairs, s // (2 * t)),
        in_specs=[pl.BlockSpec((1, 2 * t, LANES), lambda bi, pr, i: (bi, i, pr)),
                  pl.BlockSpec((1, LANES, s), lambda bi, pr, i: (bi, pr, 0)),
                  pl.BlockSpec((1, s, LANES), lambda bi, pr, i: (bi, 0, pr)),
                  pl.BlockSpec((1, 1, 2, s), lambda bi, pr, i: (bi, pr, 0, 0))],
        out_specs=pl.BlockSpec((1, 2 * t, LANES), lambda bi, pr, i: (bi, i, pr)),
        out_shape=jax.ShapeDtypeStruct((b, s, FOX_W), F32),
        scratch_shapes=[pltpu.VMEM((4, s // t, t, t), F32), pltpu.VMEM((4, t, LANES), F32),
                        pltpu.VMEM((4, t, LANES), F32), pltpu.VMEM((4, t, LANES), F32)],
        compiler_params=_params(("arbitrary", "arbitrary", "arbitrary")),
        name="fox_prompt",
    )(fqb, fktb, fvb, ct4)


def _ret_prompt_kernel(q_ref, k_ref, v_ref, rg_ref, gr_ref, dmask_ref, qdec_ref, kdec_ref, cdec_ref,
                       o_ref, st_ref):
    n = pl.program_id(1)

    @pl.when(n == 0)
    def _():
        st_ref[...] = jnp.zeros_like(st_ref)

    c = RET_CHUNK
    lane = lax.broadcasted_iota(jnp.int32, (c, LANES), 1)
    for bi in range(q_ref.shape[0]):
        for pr in range(RET_HEADS // 2):
            q = q_ref[bi, :, pr * LANES:(pr + 1) * LANES]
            k = k_ref[bi, :, pr * LANES:(pr + 1) * LANES] * (RET_DK ** -0.5)
            state = st_ref[bi, pr]
            qd = (q * qdec_ref[pr]).astype(BF16)
            kd = k * kdec_ref[pr]
            kb = k.astype(BF16)
            new_state = cdec_ref[pr] * state
            state_b = state.astype(BF16)
            for hh in range(2):
                vs = slice((2 * pr + hh) * RET_DV, (2 * pr + hh + 1) * RET_DV)
                head_lanes = (lane >= hh * RET_DK) & (lane < (hh + 1) * RET_DK)
                vb = v_ref[bi, :, vs].astype(BF16)
                qm = jnp.where(head_lanes, q, 0.0).astype(BF16)
                inner = _dot_nt(qm, kb) * dmask_ref[pr, hh]
                qdm = jnp.where(head_lanes, qd, jnp.zeros_like(qd))
                o = _dot(inner.astype(BF16), vb) + _dot(qdm, state_b)
                kdm = jnp.where(head_lanes, kd, 0.0).astype(BF16)
                new_state = new_state + _dot_tn(kdm, vb)
                mu = jnp.mean(o, axis=-1, keepdims=True)
                var = jnp.mean(jnp.square(o - mu), axis=-1, keepdims=True)
                y = (o - mu) * lax.rsqrt(var + EPS) * gr_ref[:, vs]
                rg = rg_ref[bi, :, vs]
                o_ref[bi, :, vs] = rg * _sigmoid(rg) * y
            st_ref[bi, pr] = new_state


def _ret_tables(length):
    h = RET_HEADS
    log_g = jnp.log(1.0 - 2.0 ** (-5.0 - jnp.arange(h, dtype=F32)))
    i = jnp.arange(length, dtype=F32)
    diff = i[:, None] - i[None, :]
    dmask = jnp.where(diff[None] >= 0, jnp.exp(jnp.maximum(diff, 0.0)[None] * log_g[:, None, None]), 0.0)
    q_dec = jnp.exp((i + 1.0)[:, None] * log_g[None, :])
    k_dec = jnp.exp((length - 1.0 - i)[:, None] * log_g[None, :])
    chunk_dec = jnp.exp(length * log_g)
    return dmask, q_dec, k_dec, chunk_dec


def _ret_prompt(rq, rk, rv, rg, g_ret):
    b, s, _ = rq.shape
    c = RET_CHUNK
    pairs = RET_HEADS // 2
    dmask, q_dec, k_dec, chunk_dec = _ret_tables(c)
    dmask = dmask.reshape(pairs, 2, c, c)
    qdec = jnp.repeat(q_dec, RET_DK, axis=1).reshape(c, pairs, LANES).transpose(1, 0, 2)
    kdec = jnp.repeat(k_dec, RET_DK, axis=1).reshape(c, pairs, LANES).transpose(1, 0, 2)
    cdec = jnp.repeat(chunk_dec, RET_DK).reshape(pairs, LANES, 1)
    bb = _pick_tile(b, 8)
    tok = lambda w: pl.BlockSpec((bb, c, w), lambda bi, n: (bi, n, 0))
    const = lambda shape: pl.BlockSpec(shape, lambda bi, n: (0,) * len(shape))
    o, st = pl.pallas_call(
        _ret_prompt_kernel,
        grid=(b // bb, s // c),
        in_specs=[tok(RET_QK_W), tok(RET_QK_W), tok(RET_V_W), tok(RET_V_W), const((1, RET_V_W)),
                  const((pairs, 2, c, c)), const((pairs, c, LANES)), const((pairs, c, LANES)),
                  const((pairs, LANES, 1))],
        out_specs=[tok(RET_V_W),
                   pl.BlockSpec((bb, pairs, 2 * RET_DK, RET_DV), lambda bi, n: (bi, 0, 0, 0))],
        out_shape=[jax.ShapeDtypeStruct((b, s, RET_V_W), F32),
                   jax.ShapeDtypeStruct((b, pairs, 2 * RET_DK, RET_DV), F32)],
        compiler_params=_params(("arbitrary", "arbitrary")),
        name="ret_prompt",
    )(rq, rk, rv, rg, g_ret.reshape(1, RET_V_W), dmask, qdec, kdec, cdec)
    return o, st.reshape(b, RET_HEADS, RET_DK, RET_DV)


def _ret_sample_kernel(q_ref, k_ref, v_ref, rg_ref, gr_ref, qdec_ref, cdec_ref, s0_ref, o_ref, s1_ref, *, bt):
    rows = RET_HEADS * RET_DK
    q = q_ref[...]
    k = k_ref[...] * (RET_DK ** -0.5)
    qg = q * qdec_ref[...]
    qk = q * k
    for h in range(RET_HEADS):
        v = v_ref[:, h * RET_DV:(h + 1) * RET_DV]
        cd = cdec_ref[:, h * RET_DV:(h + 1) * RET_DV]
        inner = jnp.sum(qk[:, h * RET_DK:(h + 1) * RET_DK], axis=1, keepdims=True)
        o = inner * v
        for d in range(RET_DK):
            r = h * RET_DK + d
            srow = s0_ref[pl.ds(r, bt, stride=rows), :]
            o = o + qg[:, r:r + 1] * srow
            s1_ref[pl.ds(r, bt, stride=rows), :] = cd * srow + k[:, r:r + 1] * v
        mu = jnp.mean(o, axis=-1, keepdims=True)
        var = jnp.mean(jnp.square(o - mu), axis=-1, keepdims=True)
        y = (o - mu) * lax.rsqrt(var + EPS) * gr_ref[:, h * RET_DV:(h + 1) * RET_DV]
        rg = rg_ref[:, h * RET_DV:(h + 1) * RET_DV]
        o_ref[:, h * RET_DV:(h + 1) * RET_DV] = rg * _sigmoid(rg) * y


def _ret_sample(rq, rk, rv, rg, g_ret, s0, bt):
    t = rq.shape[0]
    rows = RET_HEADS * RET_DK
    _, q_dec, _, chunk_dec = _ret_tables(1)
    qdec = jnp.repeat(q_dec, RET_DK, axis=1).reshape(1, RET_QK_W)
    cdec = jnp.repeat(chunk_dec, RET_DV).reshape(1, RET_V_W)
    tok = lambda w: pl.BlockSpec((bt, w), lambda i: (i, 0))
    const = lambda w: pl.BlockSpec((1, w), lambda i: (0, 0))
    st = pl.BlockSpec((bt * rows, RET_DV), lambda i: (i, 0))
    o, s1 = pl.pallas_call(
        functools.partial(_ret_sample_kernel, bt=bt),
        grid=(t // bt,),
        in_specs=[tok(RET_QK_W), tok(RET_QK_W), tok(RET_V_W), tok(RET_V_W), const(RET_V_W),
                  const(RET_QK_W), const(RET_V_W), st],
        out_specs=[tok(RET_V_W), st],
        out_shape=[jax.ShapeDtypeStruct((t, RET_V_W), F32), jax.ShapeDtypeStruct((t * rows, RET_DV), F32)],
        compiler_params=_params(("arbitrary",)),
        name="ret_sample",
    )(rq, rk, rv, rg, g_ret.reshape(1, RET_V_W), qdec, cdec, s0.reshape(t * rows, RET_DV))
    return o, s1.reshape(t, RET_HEADS, RET_DK, RET_DV)


def _xattn_prompt_kernel(q_ref, mk_ref, mv_ref, o_ref):
    scale = XA_HD ** -0.5
    for h in range(XA_HEADS):
        sl = slice(h * XA_HD, (h + 1) * XA_HD)
        s = _dot_nt(q_ref[0, :, sl].astype(BF16), mk_ref[0, :, sl].astype(BF16)) * scale
        p = jnp.exp(s - jnp.max(s, axis=1, keepdims=True))
        p = p / jnp.sum(p, axis=1, keepdims=True)
        o_ref[0, :, sl] = _dot(p.astype(BF16), mv_ref[0, :, sl].astype(BF16))


def _xattn_prompt(xq, mk, mv, t):
    b, s, _ = xq.shape
    m = mk.shape[1]
    return pl.pallas_call(
        _xattn_prompt_kernel,
        grid=(b, s // t),
        in_specs=[pl.BlockSpec((1, t, XA_W), lambda bi, i: (bi, i, 0)),
                  pl.BlockSpec((1, m, XA_W), lambda bi, i: (bi, 0, 0)),
                  pl.BlockSpec((1, m, XA_W), lambda bi, i: (bi, 0, 0))],
        out_specs=pl.BlockSpec((1, t, XA_W), lambda bi, i: (bi, i, 0)),
        out_shape=jax.ShapeDtypeStruct((b, s, XA_W), F32),
        compiler_params=_params(("arbitrary", "arbitrary")),
        name="xattn_prompt",
    )(xq, mk, mv)


_SUBLANES = 8


def _xattn_sample_kernel(q_ref, mk_ref, mv_ref, o_ref):
    tb, rows, _ = mk_ref.shape
    rep = _SUBLANES // XA_HEADS
    groups = rows // _SUBLANES

    def fold(x, op):
        out = x[:XA_HEADS]
        for r in range(1, rep):
            out = op(out, x[r * XA_HEADS:(r + 1) * XA_HEADS])
        return out

    for b in range(tb):
        q = q_ref[b]
        qt = jnp.concatenate([q] * rep, axis=0)[None]
        mk = mk_ref[b].reshape(groups, _SUBLANES, XA_HD)
        mv = mv_ref[b].reshape(groups, _SUBLANES, XA_HD)
        s = jnp.sum(mk * qt, axis=2, keepdims=True) * (XA_HD ** -0.5)
        m = fold(jnp.max(s, axis=0), jnp.maximum)
        p = jnp.exp(s - jnp.concatenate([m] * rep, axis=0)[None])
        l = fold(jnp.sum(p, axis=0), jnp.add)
        acc = fold(jnp.sum(p * mv, axis=0), jnp.add)
        o_ref[b] = acc / l


def _xattn_sample(xq, mk, mv, tb):
    t, rows, _ = mk.shape
    return pl.pallas_call(
        _xattn_sample_kernel,
        grid=(t // tb,),
        in_specs=[pl.BlockSpec((tb, XA_HEADS, XA_HD), lambda i: (i, 0, 0)),
                  pl.BlockSpec((tb, rows, XA_HD), lambda i: (i, 0, 0)),
                  pl.BlockSpec((tb, rows, XA_HD), lambda i: (i, 0, 0))],
        out_specs=pl.BlockSpec((tb, XA_HEADS, XA_HD), lambda i: (i, 0, 0)),
        out_shape=jax.ShapeDtypeStruct((t, XA_HEADS, XA_HD), F32),
        compiler_params=_params(("arbitrary",)),
        name="xattn_sample",
    )(xq, mk, mv)


_DECODE_PAGES = 16
_DECODE_SLOTS = 4


def _suffix_sum_lanes(x):
    n = x.shape[1]
    lane = lax.broadcasted_iota(jnp.int32, x.shape, 1)
    sh = 1
    while sh < n:
        x = x + jnp.where(lane < n - sh, pltpu.roll(x, n - sh, 1), 0.0)
        sh *= 2
    return x


def _fox_sample_kernel(pt_ref, q_ref, cn_ref, kn_ref, vn_ref, ck_hbm, clf_hbm, cv_hbm, o_ref,
                       buf_ref, lfbuf_ref, sem, lfsem, qb_ref, part_ref, p_ref, s_ref, m_ref, sn_ref, cnb_ref,
                       carry_ref, acc_ref, l_ref, *, pps, n_steps, n_pages, n_tok, n_slots):
    hrow = lax.broadcasted_iota(jnp.int32, (FOX_HEADS, FOX_W), 0)
    hlane = lax.broadcasted_iota(jnp.int32, (FOX_HEADS, FOX_W), 1)
    head_mask = (hlane >= hrow * FOX_HD) & (hlane < (hrow + 1) * FOX_HD)

    def rows_to_col(row_vals):
        r8 = lax.broadcasted_iota(jnp.int32, (FOX_HEADS, FOX_HEADS), 0)
        c8 = lax.broadcasted_iota(jnp.int32, (FOX_HEADS, FOX_HEADS), 1)
        b = jnp.broadcast_to(row_vals, (FOX_HEADS, FOX_HEADS))
        return jnp.sum(jnp.where(r8 == c8, b, 0.0), axis=1, keepdims=True)

    def col_to_head_row(col_vals):
        b = jnp.broadcast_to(col_vals, (FOX_HEADS, FOX_W))
        return jnp.sum(jnp.where(head_mask, b, 0.0), axis=0, keepdims=True)

    def unrolled(n, body, init=0):
        return lax.fori_loop(0, n, body, init, unroll=True)

    def static_when(cond):
        def run(f):
            if cond:
                f()
        return run

    def page_copies(keys, b, j, slot, i):
        page = 0 if b is None else pt_ref[b, n_pages - 1 - (j * pps + i)]
        if keys:
            return (pltpu.make_async_copy(ck_hbm.at[page], buf_ref.at[slot, i], sem.at[slot]),
                    pltpu.make_async_copy(clf_hbm.at[page], lfbuf_ref.at[slot, i], lfsem.at[slot]))
        return (pltpu.make_async_copy(cv_hbm.at[page], buf_ref.at[slot, i], sem.at[slot]),)

    def start_chunk(keys, b, j, slot):
        def body(i, c):
            for cp in page_copies(keys, b, j, slot, i):
                cp.start()
            return c
        unrolled(pps, body)

    def wait_chunk(keys, slot):
        def body(i, c):
            for cp in page_copies(keys, None, 0, slot, i):
                cp.wait()
            return c
        unrolled(pps, body)

    def key_chunk(b, j, slot):
        @static_when(j == 0)
        def _():
            q = q_ref[pl.ds(b, 1), :]
            qb_ref[...] = jnp.transpose(jnp.broadcast_to(q, (LANES, FOX_W)))
            cnb_ref[...] = jnp.broadcast_to(rows_to_col(cn_ref[pl.ds(b, 1), :]), cnb_ref.shape)
            carry_ref[...] = jnp.zeros_like(carry_ref)
            m_ref[...] = jnp.full_like(m_ref, NEG_INF)

        def head(h, c):
            qbh = qb_ref[pl.ds(pl.multiple_of(h * FOX_HD, FOX_HD), FOX_HD), :]

            def page_part(i, c2):
                prod = buf_ref[slot, i, h] * qbh
                part_ref[i, h] = jnp.sum(prod.reshape(FOX_HD // 8, 8, LANES), axis=0)
                return c2
            return unrolled(pps, page_part, c)
        unrolled(FOX_HEADS, head)

        cnb = cnb_ref[...]

        def page_scores(i, mc):
            m, carry = mc
            lf = lfbuf_ref[slot, i]
            suf = _suffix_sum_lanes(lf)
            s = jnp.sum(part_ref[i], axis=1) + cnb + ((suf - lf) + carry)
            s_ref[j * pps + i] = s
            return jnp.maximum(m, s), carry + suf[:, 0:1]
        m, carry = unrolled(pps, page_scores, (m_ref[...], carry_ref[...]))
        carry_ref[...] = carry
        m_ref[...] = m

        @static_when(j == n_steps - 1)
        def _():
            q8 = jnp.where(head_mask, jnp.broadcast_to(q_ref[pl.ds(b, 1), :], (FOX_HEADS, FOX_W)), 0.0)
            sn = jnp.sum(q8 * kn_ref[pl.ds(b, 1), :], axis=1, keepdims=True)
            mf = jnp.maximum(jnp.max(m, axis=1, keepdims=True), sn)
            sn_ref[...] = jnp.broadcast_to(sn, sn_ref.shape)
            m_ref[...] = jnp.broadcast_to(mf, m_ref.shape)

    def value_chunk(b, j, slot):
        @static_when(j == 0)
        def _():
            acc_ref[...] = jnp.zeros_like(acc_ref)
            l_ref[...] = jnp.zeros_like(l_ref)

        m = m_ref[...]

        def page_probs(i, l):
            p = jnp.exp(s_ref[j * pps + i] - m)
            p_ref[i] = p
            return l + p
        l = unrolled(pps, page_probs, l_ref[...])
        l_ref[...] = l

        def head(h, c):
            def page_acc(i, a):
                return a + p_ref[i, pl.ds(h, 1), :] * buf_ref[slot, i, h]
            acc_ref[h] = unrolled(pps, page_acc, acc_ref[h])
            return c
        unrolled(FOX_HEADS, head)

        @static_when(j == n_steps - 1)
        def _():
            pn = jnp.exp(sn_ref[:, 0:1] - m[:, 0:1])
            lsum = jnp.sum(l, axis=1, keepdims=True) + pn
            acc_t = jnp.transpose(acc_ref[...].reshape(FOX_W, LANES))
            acc = jnp.sum(acc_t, axis=0, keepdims=True) + col_to_head_row(pn) * vn_ref[pl.ds(b, 1), :]
            o_ref[pl.ds(b, 1), :] = acc / col_to_head_row(lsum)

    n_chunks = 2 * n_steps
    ahead = n_slots - 1

    def start_ahead(b, r):
        slot = r % n_slots
        if r < n_chunks:
            start_chunk(r < n_steps, b, r % n_steps, slot)
        else:
            @pl.when(b + 1 < n_tok)
            def _():
                start_chunk(True, b + 1, r - n_chunks, slot)

    for r in range(ahead):
        start_chunk(True, 0, r, r % n_slots)

    def token(b, carry):
        for r in range(n_chunks):
            slot = r % n_slots
            wait_chunk(r < n_steps, slot)
            start_ahead(b, r + ahead)
            if r < n_steps:
                key_chunk(b, r, slot)
            else:
                value_chunk(b, r - n_steps, slot)
        return carry

    lax.fori_loop(0, n_tok, token, 0)


def _fox_sample(page_table, fq, logf, fk, fv, ckt, cvt, clft, pps):
    t, n_pages = page_table.shape
    page = ckt.shape[3]
    assert page == LANES
    n_steps = n_pages // pps
    n_slots = _DECODE_SLOTS if (2 * n_steps) % _DECODE_SLOTS == 0 and _DECODE_SLOTS - 1 <= n_steps else 2
    full = lambda w: pl.BlockSpec((t, w), lambda i, pt: (0, 0))
    hbm = pl.BlockSpec(memory_space=pl.ANY)
    hb = (FOX_HEADS, LANES)
    grid_spec = pltpu.PrefetchScalarGridSpec(
        num_scalar_prefetch=1,
        grid=(1,),
        in_specs=[full(FOX_W), full(FOX_HEADS), full(FOX_W), full(FOX_W), hbm, hbm, hbm],
        out_specs=full(FOX_W),
        scratch_shapes=[pltpu.VMEM((n_slots, pps, FOX_HEADS, FOX_HD, LANES), F32),
                        pltpu.VMEM((n_slots, pps) + hb, F32),
                        pltpu.SemaphoreType.DMA((n_slots,)), pltpu.SemaphoreType.DMA((n_slots,)),
                        pltpu.VMEM((FOX_W, LANES), F32),
                        pltpu.VMEM((pps, FOX_HEADS, 8, LANES), F32),
                        pltpu.VMEM((pps,) + hb, F32),
                        pltpu.VMEM((n_pages,) + hb, F32),
                        pltpu.VMEM(hb, F32), pltpu.VMEM(hb, F32), pltpu.VMEM(hb, F32), pltpu.VMEM(hb, F32),
                        pltpu.VMEM((FOX_HEADS, FOX_HD, LANES), F32), pltpu.VMEM(hb, F32)],
    )
    return pl.pallas_call(
        functools.partial(_fox_sample_kernel, pps=pps, n_steps=n_steps, n_pages=n_pages, n_tok=t,
                          n_slots=n_slots),
        grid_spec=grid_spec,
        out_shape=jax.ShapeDtypeStruct((t, FOX_W), F32),
        compiler_params=_params(("arbitrary",)),
        name="fox_sample",
    )(page_table, fq, logf, fk, fv, ckt, clft, cvt)


def _memkv_kernel(x_ref, g_ref, w_ref, mk_ref, mv_ref):
    hn = _rms(x_ref[...], g_ref[...]).astype(BF16)
    mk_ref[...] = _dot(hn, w_ref[:, :XA_W])
    mv_ref[...] = _dot(hn, w_ref[:, XA_W:])


def _memkv(mem, g_mem, w_kv, tm):
    rows = mem.shape[0]
    return pl.pallas_call(
        _memkv_kernel,
        grid=(rows // tm,),
        in_specs=[pl.BlockSpec((tm, D_MODEL), lambda i: (i, 0)),
                  pl.BlockSpec((1, D_MODEL), lambda i: (0, 0)),
                  pl.BlockSpec((D_MODEL, 2 * XA_W), lambda i: (0, 0))],
        out_specs=[pl.BlockSpec((tm, XA_W), lambda i: (i, 0))] * 2,
        out_shape=[jax.ShapeDtypeStruct((rows, XA_W), F32)] * 2,
        compiler_params=_params(("arbitrary",)),
        name="memkv",
    )(mem, g_mem.reshape(1, D_MODEL), w_kv)


_FF_CHUNK = 1408


def _post_kernel(x_ref, gates_ref, of_ref, or_ref, ox_ref, wf_ref, wr_ref, wx_ref, wo_ref, gffn_ref,
                 wgu_ref, wd_ref, gfin_ref, y_ref):
    mix = None
    for idx, (o_ref, w_ref) in enumerate(((of_ref, wf_ref), (or_ref, wr_ref), (ox_ref, wx_ref))):
        gate = _sigmoid(gates_ref[:, idx * D_MODEL:(idx + 1) * D_MODEL])
        term = gate * _dot(o_ref[...].astype(BF16), w_ref[...])
        mix = term if mix is None else mix + term
    x = x_ref[...] + _dot(mix.astype(BF16), wo_ref[...])
    hb = _rms(x, gffn_ref[...]).astype(BF16)
    ffn = jnp.zeros_like(x)
    for c in range(0, D_FF, _FF_CHUNK):
        u_gate = _dot(hb, wgu_ref[:, c:c + _FF_CHUNK])
        u_up = _dot(hb, wgu_ref[:, D_FF + c:D_FF + c + _FF_CHUNK])
        act = (u_gate * _sigmoid(u_gate) * u_up).astype(BF16)
        ffn = ffn + _dot(act, wd_ref[c:c + _FF_CHUNK, :])
    y_ref[...] = _rms(x + ffn, gfin_ref[...])


def _post(x, gates, o_fox, o_ret, o_xa, wf, wr, wx, wo, g_ffn, wgu, wd, g_final, tm):
    rows = x.shape[0]
    row = lambda w: pl.BlockSpec((tm, w), lambda i: (i, 0))
    const = lambda a, b: pl.BlockSpec((a, b), lambda i: (0, 0))
    return pl.pallas_call(
        _post_kernel,
        grid=(rows // tm,),
        in_specs=[row(D_MODEL), row(3 * D_MODEL), row(FOX_W), row(RET_V_W), row(XA_W),
                  const(FOX_W, D_MODEL), const(RET_V_W, D_MODEL), const(XA_W, D_MODEL),
                  const(D_MODEL, D_MODEL), const(1, D_MODEL), const(D_MODEL, 2 * D_FF),
                  const(D_FF, D_MODEL), const(1, D_MODEL)],
        out_specs=row(D_MODEL),
        out_shape=jax.ShapeDtypeStruct((rows, D_MODEL), F32),
        compiler_params=_params(("arbitrary",)),
        name="post",
    )(x, gates, o_fox, o_ret, o_xa, wf, wr, wx, wo, g_ffn.reshape(1, D_MODEL), wgu, wd,
      g_final.reshape(1, D_MODEL))


def _rotary_tables(pos):
    half = RET_DK // 2
    inv = ROPE_BASE ** (-jnp.arange(half, dtype=F32) / half)
    ang = pos.astype(F32)[:, None] * inv[None, :]
    cos, sin = jnp.cos(ang), jnp.sin(ang)
    zero = jnp.zeros_like(sin)
    tile = lambda a, b: jnp.tile(jnp.concatenate([a, b], axis=1), (1, RET_HEADS))
    return tile(cos, cos), tile(-sin, zero), tile(zero, sin)


def _pick_tile(n, pref):
    t = min(n, pref)
    while n % t:
        t //= 2
    return t


def kernel(x_prompt, x_sample, mem_prompt, cache_fox_k, cache_fox_v, cache_fox_logf, state_ret, cache_mem_k, cache_mem_v, page_table, g_attn, w_in, b_f, g_ret, w_br_fox, w_br_ret, w_br_xa, w_o, g_ffn, w_gu, w_down, g_mem, w_mem_kv, g_final):
    bp, sp, _ = x_prompt.shape
    bs, ts, _ = x_sample.shape
    depth = w_in.shape[0]
    assert depth == 1 and ts == 1
    n_pages, page = page_table.shape[1], cache_fox_k.shape[2]
    past = n_pages * page
    n_mem = mem_prompt.shape[1]
    l = 0

    flog0 = 3 * FOX_W
    w_l = w_in[l]
    w_main = jnp.concatenate([w_l[:, :flog0], w_l[:, flog0 + FOX_HEADS:]], axis=1).astype(BF16)
    w_kvt = w_l[:, FOX_W:flog0].T.astype(BF16)
    w_flt = w_l[:, flog0:flog0 + FOX_HEADS].T.astype(BF16)
    wf, wr, wx = w_br_fox[l].astype(BF16), w_br_ret[l].astype(BF16), w_br_xa[l].astype(BF16)
    wo, wgu, wd = w_o[l].astype(BF16), w_gu[l].astype(BF16), w_down[l].astype(BF16)
    w_kv = w_mem_kv[l].astype(BF16)

    def token_major(a_t, n_heads, head_w):
        b, _, s = a_t.shape
        return a_t.reshape(b, n_heads, head_w, s).transpose(0, 3, 1, 2)[None]

    mk_p, mv_p = _memkv(mem_prompt.reshape(bp * n_mem, D_MODEL), g_mem[l], w_kv, _pick_tile(bp * n_mem, 256))
    mk_p = mk_p.reshape(bp, n_mem, XA_W)
    mv_p = mv_p.reshape(bp, n_mem, XA_W)
    tm = _pick_tile(sp, 256)
    cos, s1, s2 = _rotary_tables(jnp.arange(sp))
    (gates, fqb, fkt, fvt, fktb, fvb, lft, ct, rq, rk, rv, rg, xq) = _inproj(
        x_prompt, g_attn[l], w_main, w_kvt, w_flt, b_f[l], cos, s1, s2, tm, False)
    o_fox = _fox_prompt(fqb, fktb, fvb, ct, _pick_tile(sp // 2, 512))
    o_ret, st_p = _ret_prompt(rq, rk, rv, rg, g_ret[l])
    o_xa = _xattn_prompt(xq, mk_p, mv_p, _pick_tile(sp, 512))
    rows = bp * sp
    y_p = _post(x_prompt.reshape(rows, D_MODEL), gates.reshape(rows, 3 * D_MODEL), o_fox.reshape(rows, FOX_W),
                o_ret.reshape(rows, RET_V_W), o_xa.reshape(rows, XA_W), wf, wr, wx, wo, g_ffn[l], wgu, wd,
                g_final, _pick_tile(rows, 256)).reshape(bp, sp, D_MODEL)

    cos_s, s1_s, s2_s = _rotary_tables(jnp.full((bs,), past, jnp.int32))
    (gates_s, fqb_s, fkt_s, fvt_s, _, fvb_s, lft_s, _, rq_s, rk_s, rv_s, rg_s, xq_s, fkb_s) = _inproj(
        x_sample.reshape(1, bs, D_MODEL), g_attn[l], w_main, w_kvt, w_flt, b_f[l], cos_s, s1_s, s2_s, bs, True)
    two = lambda a: a.reshape(bs, a.shape[-1])
    ckt = cache_fox_k[l].transpose(0, 2, 3, 1)
    cvt = cache_fox_v[l].transpose(0, 2, 3, 1)
    clft = cache_fox_logf[l].transpose(0, 2, 1)
    o_fox_s = _fox_sample(page_table, two(fqb_s).astype(F32), lft_s[0].T, two(fkb_s).astype(F32),
                          two(fvb_s).astype(F32), ckt, cvt, clft, _pick_tile(n_pages, _DECODE_PAGES))
    o_ret_s, st_s = _ret_sample(two(rq_s), two(rk_s), two(rv_s), two(rg_s), g_ret[l], state_ret[l],
                                _pick_tile(bs, 32))
    o_xa_s = _xattn_sample(two(xq_s).reshape(bs, XA_HEADS, XA_HD),
                           cache_mem_k[l].reshape(bs, n_mem * XA_HEADS, XA_HD),
                           cache_mem_v[l].reshape(bs, n_mem * XA_HEADS, XA_HD),
                           _pick_tile(bs, 8)).reshape(bs, XA_W)
    y_s = _post(x_sample.reshape(bs, D_MODEL), two(gates_s), o_fox_s, o_ret_s, o_xa_s, wf, wr, wx, wo,
                g_ffn[l], wgu, wd, g_final, bs).reshape(bs, 1, D_MODEL)

    stack = lambda a, shape: a.reshape((1,) + shape)
    sample_major = lambda a_t, n_heads, head_w: (
        a_t.reshape(n_heads, head_w, bs).transpose(2, 0, 1).reshape(1, bs, 1, n_heads, head_w))
    return (y_p, y_s,
            token_major(fkt, FOX_HEADS, FOX_HD), token_major(fvt, FOX_HEADS, FOX_HD),
            lft.transpose(0, 2, 1)[None], stack(st_p, (bp, RET_HEADS, RET_DK, RET_DV)),
            stack(mk_p, (bp, n_mem, XA_HEADS, XA_HD)), stack(mv_p, (bp, n_mem, XA_HEADS, XA_HD)),
            sample_major(fkt_s, FOX_HEADS, FOX_HD), sample_major(fvt_s, FOX_HEADS, FOX_HD),
            lft_s[0].T.reshape(1, bs, 1, FOX_HEADS), stack(st_s, (bs, RET_HEADS, RET_DK, RET_DV)))
```

```python
import functools

import jax
import jax.numpy as jnp
from jax import lax
from jax.experimental import pallas as pl
from jax.experimental.pallas import tpu as pltpu

D_MODEL = 1024
FOX_HEADS = 8
FOX_HD = 64
RET_HEADS = 4
RET_DK = 64
RET_DV = 128
XA_HEADS = 4
XA_HD = 128
FOX_W = FOX_HEADS * FOX_HD
RET_QK_W = RET_HEADS * RET_DK
RET_V_W = RET_HEADS * RET_DV
XA_W = XA_HEADS * XA_HD
D_FF = 2816
RET_CHUNK = 128
ROPE_BASE = 10000.0
EPS = 1e-6
NEG_INF = -1e30

LANES = 128
VMEM_LIMIT = 56 * 1024 * 1024

F32 = jnp.float32
BF16 = jnp.bfloat16

_C_FQ, _C_FK, _C_FV = 0, 512, 1024
_C_RQK, _C_RV, _C_RG, _C_XQ, _C_GATES = 1536, 2048, 2560, 3072, 3584
_W_MAIN = 6656


def _params(sem):
    return pltpu.CompilerParams(dimension_semantics=sem, vmem_limit_bytes=VMEM_LIMIT)


def _dot(a, b):
    return jnp.dot(a, b, preferred_element_type=F32)


def _dot_nt(a, b):
    return lax.dot_general(a, b, (((1,), (1,)), ((), ())), preferred_element_type=F32)


def _dot_tn(a, b):
    return lax.dot_general(a, b, (((0,), (0,)), ((), ())), preferred_element_type=F32)


def _rms(x, g):
    return x * lax.rsqrt(jnp.mean(x * x, axis=-1, keepdims=True) + EPS) * g


def _log_sigmoid(x):
    return -(jnp.maximum(-x, 0.0) + jnp.log1p(jnp.exp(-jnp.abs(x))))


def _sigmoid(x):
    return 1.0 / (1.0 + jnp.exp(-x))


def _split3(x):
    hi = x.astype(BF16)
    r1 = x - hi.astype(F32)
    mid = r1.astype(BF16)
    lo = (r1 - mid.astype(F32)).astype(BF16)
    return hi, mid, lo


def _inproj_kernel(x_ref, g_ref, wm_ref, wkvt_ref, wflt_ref, bft_ref, cos_ref, s1_ref, s2_ref,
                   gates_ref, fq_ref, fkt_ref, fvt_ref, fktb_ref, fvb_ref, lft_ref, ct_ref,
                   rq_ref, rk_ref, rv_ref, rg_ref, xq_ref, *tail, tm):
    carry_ref = tail[-1]
    j = pl.program_id(1)
    hn = _rms(x_ref[0], g_ref[...]).astype(BF16)

    def proj(c0, width):
        return _dot(hn, wm_ref[:, c0:c0 + width])

    fq_ref[0] = (proj(_C_FQ, FOX_W) * (FOX_HD ** -0.5)).astype(BF16)
    if len(tail) == 2:
        tail[0][0] = proj(_C_FK, FOX_W).astype(BF16)
    fvb_ref[0] = proj(_C_FV, FOX_W).astype(BF16)
    fkt = _dot_nt(wkvt_ref[:FOX_W, :], hn)
    fkt_ref[0] = fkt
    fktb_ref[0] = fkt.astype(BF16)
    fvt_ref[0] = _dot_nt(wkvt_ref[FOX_W:, :], hn)

    rqk = proj(_C_RQK, 2 * RET_QK_W)
    cos, s1, s2 = cos_ref[...], s1_ref[...], s2_ref[...]
    for ref, off in ((rq_ref, 0), (rk_ref, RET_QK_W)):
        r = rqk[:, off:off + RET_QK_W]
        ref[0] = (r * cos + pltpu.roll(r, RET_QK_W - RET_DK // 2, 1) * s1
                  + pltpu.roll(r, RET_DK // 2, 1) * s2)
    rv_ref[0] = proj(_C_RV, RET_V_W)
    rg_ref[0] = proj(_C_RG, RET_V_W)
    xq_ref[0] = proj(_C_XQ, XA_W)
    for c in range(0, 3 * D_MODEL, D_MODEL):
        gates_ref[0, :, c:c + D_MODEL] = proj(_C_GATES + c, D_MODEL)

    lft = _log_sigmoid(_dot_nt(wflt_ref[...], hn) + bft_ref[...])
    lft_ref[0] = lft

    @pl.when(j == 0)
    def _():
        carry_ref[...] = jnp.zeros_like(carry_ref)

    row = lax.broadcasted_iota(jnp.int32, (tm, tm), 0)
    col = lax.broadcasted_iota(jnp.int32, (tm, tm), 1)
    tri = (row <= col).astype(BF16)
    hi, mid, lo = _split3(lft)
    ct = _dot(hi, tri) + _dot(mid, tri) + _dot(lo, tri) + carry_ref[:, 0:1]
    ct_ref[0] = ct
    carry_ref[...] = jnp.broadcast_to(ct[:, tm - 1:tm], carry_ref.shape)


def _inproj(x, g_attn, w_main, w_kvt, w_flt, b_f, cos, s1, s2, tm, token_major_k):
    b, s, _ = x.shape
    grid = (b, s // tm)
    tok = lambda w: pl.BlockSpec((1, tm, w), lambda i, j: (i, j, 0))
    feat = lambda h: pl.BlockSpec((1, h, tm), lambda i, j: (i, 0, j))
    const = lambda shape: pl.BlockSpec(shape, lambda i, j: (0,) * len(shape))
    tab = pl.BlockSpec((tm, RET_QK_W), lambda i, j: (j, 0))
    outs = [
        (tok, 3 * D_MODEL, F32), (tok, FOX_W, BF16),
        (feat, FOX_W, F32), (feat, FOX_W, F32), (feat, FOX_W, BF16),
        (tok, FOX_W, BF16),
        (feat, FOX_HEADS, F32), (feat, FOX_HEADS, F32),
        (tok, RET_QK_W, F32), (tok, RET_QK_W, F32), (tok, RET_V_W, F32), (tok, RET_V_W, F32), (tok, XA_W, F32),
    ] + ([(tok, FOX_W, BF16)] if token_major_k else [])
    shape = lambda kind, w: (b, s, w) if kind is tok else (b, w, s)
    return pl.pallas_call(
        functools.partial(_inproj_kernel, tm=tm),
        grid=grid,
        in_specs=[tok(D_MODEL), const((1, D_MODEL)), const((D_MODEL, _W_MAIN)), const((2 * FOX_W, D_MODEL)),
                  const((FOX_HEADS, D_MODEL)), const((FOX_HEADS, 1)), tab, tab, tab],
        out_specs=[kind(w) for kind, w, _ in outs],
        out_shape=[jax.ShapeDtypeStruct(shape(kind, w), dt) for kind, w, dt in outs],
        scratch_shapes=[pltpu.VMEM((FOX_HEADS, LANES), F32)],
        compiler_params=_params(("arbitrary", "arbitrary")),
        name="inproj",
    )(x, g_attn.reshape(1, D_MODEL), w_main, w_kvt, w_flt, b_f.reshape(FOX_HEADS, 1), cos, s1, s2)


def _lane_blocks(x, op):
    out = x[:, :LANES]
    for c in range(LANES, x.shape[1], LANES):
        out = op(out, x[:, c:c + LANES])
    return out


_LOG2E = 1.4426950408889634


def _fox_prompt_kernel(q_ref, kt_ref, v_ref, ct_ref, o_ref, s_ref, mx_ref, l_ref, acc_ref, *, t):
    i = pl.program_id(2)
    lane = lax.broadcasted_iota(jnp.int32, (t, LANES), 1)
    row = lax.broadcasted_iota(jnp.int32, (t, t), 0)
    col = lax.broadcasted_iota(jnp.int32, (t, t), 1)
    chains = [(hh, half) for hh in range(2) for half in range(2)]
    qm = []
    for hh, half in chains:
        q = q_ref[0, half * t:(half + 1) * t, :]
        head_lanes = (lane >= hh * FOX_HD) & (lane < (hh + 1) * FOX_HD)
        qm.append(jnp.where(head_lanes, q, jnp.zeros_like(q)))

    def score_tile(c, j, diag):
        hh = chains[c][0]
        off = pl.multiple_of(j * t, t)
        s = (_dot(qm[c], kt_ref[0, :, pl.ds(off, t)]) - ct_ref[0, 0, hh:hh + 1, pl.ds(off, t)]) * _LOG2E
        if diag:
            s = jnp.where(col <= row, s, NEG_INF)
        s_ref[c, j] = s
        mx_ref[c] = jnp.maximum(mx_ref[c], _lane_blocks(s, jnp.maximum))

    tail = {0: ((0, True),), 1: ((0, False), (1, True))}

    mx_ref[...] = jnp.full_like(mx_ref, NEG_INF)

    def pass1(j, carry):
        for c in range(4):
            score_tile(c, j, False)
        return carry

    lax.fori_loop(0, 2 * i, pass1, 0)
    for c, (_, half) in enumerate(chains):
        for dj, diag in tail[half]:
            score_tile(c, 2 * i + dj, diag)

    m = [jnp.max(mx_ref[c], axis=1, keepdims=True) for c in range(4)]
    l_ref[...] = jnp.zeros_like(l_ref)
    acc_ref[...] = jnp.zeros_like(acc_ref)

    def value_tile(c, j):
        off = pl.multiple_of(j * t, t)
        p = jnp.exp2(s_ref[c, j] - m[c])
        l_ref[c] += _lane_blocks(p, jnp.add)
        acc_ref[c] += _dot(p.astype(BF16), v_ref[0, pl.ds(off, t), :])

    def pass2(j, carry):
        for c in range(4):
            value_tile(c, j)
        return carry

    lax.fori_loop(0, 2 * i, pass2, 0)
    for c, (_, half) in enumerate(chains):
        for dj, _ in tail[half]:
            value_tile(c, 2 * i + dj)

    out = [acc_ref[c] / jnp.sum(l_ref[c], axis=1, keepdims=True) for c in range(4)]
    for half in range(2):
        o_ref[0, half * t:(half + 1) * t, :] = jnp.where(lane < FOX_HD, out[half], out[2 + half])


def _fox_prompt(fqb, fktb, fvb, ct, t):
    b, s, _ = fqb.shape
    pairs = FOX_HEADS // 2
    ct4 = ct.reshape(b, pairs, 2, s)
    return pl.pallas_call(
        functools.partial(_fox_prompt_kernel, t=t),
        grid=(b, pairs, s // (2 * t)),
        in_specs=[pl.BlockSpec((1, 2 * t, LANES), lambda bi, pr, i: (bi, i, pr)),
                  pl.BlockSpec((1, LANES, s), lambda bi, pr, i: (bi, pr, 0)),
                  pl.BlockSpec((1, s, LANES), lambda bi, pr, i: (bi, 0, pr)),
                  pl.BlockSpec((1, 1, 2, s), lambda bi, pr, i: (bi, pr, 0, 0))],
        out_specs=pl.BlockSpec((1, 2 * t, LANES), lambda bi, pr, i: (bi, i, pr)),
        out_shape=jax.ShapeDtypeStruct((b, s, FOX_W), F32),
        scratch_shapes=[pltpu.VMEM((4, s // t, t, t), F32), pltpu.VMEM((4, t, LANES), F32),
                        pltpu.VMEM((4, t, LANES), F32), pltpu.VMEM((4, t, LANES), F32)],
        compiler_params=_params(("arbitrary", "arbitrary", "arbitrary")),
        name="fox_prompt",
    )(fqb, fktb, fvb, ct4)


def _ret_prompt_kernel(q_ref, k_ref, v_ref, rg_ref, gr_ref, dmask_ref, qdec_ref, kdec_ref, cdec_ref,
                       o_ref, st_ref):
    n = pl.program_id(1)

    @pl.when(n == 0)
    def _():
        st_ref[...] = jnp.zeros_like(st_ref)

    c = RET_CHUNK
    lane = lax.broadcasted_iota(jnp.int32, (c, LANES), 1)
    for bi in range(q_ref.shape[0]):
        for pr in range(RET_HEADS // 2):
            q = q_ref[bi, :, pr * LANES:(pr + 1) * LANES]
            k = k_ref[bi, :, pr * LANES:(pr + 1) * LANES] * (RET_DK ** -0.5)
            state = st_ref[bi, pr]
            qd = (q * qdec_ref[pr]).astype(BF16)
            kd = k * kdec_ref[pr]
            kb = k.astype(BF16)
            new_state = cdec_ref[pr] * state
            state_b = state.astype(BF16)
            for hh in range(2):
                vs = slice((2 * pr + hh) * RET_DV, (2 * pr + hh + 1) * RET_DV)
                head_lanes = (lane >= hh * RET_DK) & (lane < (hh + 1) * RET_DK)
                vb = v_ref[bi, :, vs].astype(BF16)
                qm = jnp.where(head_lanes, q, 0.0).astype(BF16)
                inner = _dot_nt(qm, kb) * dmask_ref[pr, hh]
                qdm = jnp.where(head_lanes, qd, jnp.zeros_like(qd))
                o = _dot(inner.astype(BF16), vb) + _dot(qdm, state_b)
                kdm = jnp.where(head_lanes, kd, 0.0).astype(BF16)
                new_state = new_state + _dot_tn(kdm, vb)
                mu = jnp.mean(o, axis=-1, keepdims=True)
                var = jnp.mean(jnp.square(o - mu), axis=-1, keepdims=True)
                y = (o - mu) * lax.rsqrt(var + EPS) * gr_ref[:, vs]
                rg = rg_ref[bi, :, vs]
                o_ref[bi, :, vs] = rg * _sigmoid(rg) * y
            st_ref[bi, pr] = new_state


def _ret_tables(length):
    h = RET_HEADS
    log_g = jnp.log(1.0 - 2.0 ** (-5.0 - jnp.arange(h, dtype=F32)))
    i = jnp.arange(length, dtype=F32)
    diff = i[:, None] - i[None, :]
    dmask = jnp.where(diff[None] >= 0, jnp.exp(jnp.maximum(diff, 0.0)[None] * log_g[:, None, None]), 0.0)
    q_dec = jnp.exp((i + 1.0)[:, None] * log_g[None, :])
    k_dec = jnp.exp((length - 1.0 - i)[:, None] * log_g[None, :])
    chunk_dec = jnp.exp(length * log_g)
    return dmask, q_dec, k_dec, chunk_dec


def _ret_prompt(rq, rk, rv, rg, g_ret):
    b, s, _ = rq.shape
    c = RET_CHUNK
    pairs = RET_HEADS // 2
    dmask, q_dec, k_dec, chunk_dec = _ret_tables(c)
    dmask = dmask.reshape(pairs, 2, c, c)
    qdec = jnp.repeat(q_dec, RET_DK, axis=1).reshape(c, pairs, LANES).transpose(1, 0, 2)
    kdec = jnp.repeat(k_dec, RET_DK, axis=1).reshape(c, pairs, LANES).transpose(1, 0, 2)
    cdec = jnp.repeat(chunk_dec, RET_DK).reshape(pairs, LANES, 1)
    bb = _pick_tile(b, 8)
    tok = lambda w: pl.BlockSpec((bb, c, w), lambda bi, n: (bi, n, 0))
    const = lambda shape: pl.BlockSpec(shape, lambda bi, n: (0,) * len(shape))
    o, st = pl.pallas_call(
        _ret_prompt_kernel,
        grid=(b // bb, s // c),
        in_specs=[tok(RET_QK_W), tok(RET_QK_W), tok(RET_V_W), tok(RET_V_W), const((1, RET_V_W)),
                  const((pairs, 2, c, c)), const((pairs, c, LANES)), const((pairs, c, LANES)),
                  const((pairs, LANES, 1))],
        out_specs=[tok(RET_V_W),
                   pl.BlockSpec((bb, pairs, 2 * RET_DK, RET_DV), lambda bi, n: (bi, 0, 0, 0))],
        out_shape=[jax.ShapeDtypeStruct((b, s, RET_V_W), F32),
                   jax.ShapeDtypeStruct((b, pairs, 2 * RET_DK, RET_DV), F32)],
        compiler_params=_params(("arbitrary", "arbitrary")),
        name="ret_prompt",
    )(rq, rk, rv, rg, g_ret.reshape(1, RET_V_W), dmask, qdec, kdec, cdec)
    return o, st.reshape(b, RET_HEADS, RET_DK, RET_DV)


def _ret_sample_kernel(q_ref, k_ref, v_ref, rg_ref, gr_ref, qdec_ref, cdec_ref, s0_ref, o_ref, s1_ref, *, bt):
    rows = RET_HEADS * RET_DK
    q = q_ref[...]
    k = k_ref[...] * (RET_DK ** -0.5)
    qg = q * qdec_ref[...]
    qk = q * k
    for h in range(RET_HEADS):
        v = v_ref[:, h * RET_DV:(h + 1) * RET_DV]
        cd = cdec_ref[:, h * RET_DV:(h + 1) * RET_DV]
        inner = jnp.sum(qk[:, h * RET_DK:(h + 1) * RET_DK], axis=1, keepdims=True)
        o = inner * v
        for d in range(RET_DK):
            r = h * RET_DK + d
            srow = s0_ref[pl.ds(r, bt, stride=rows), :]
            o = o + qg[:, r:r + 1] * srow
            s1_ref[pl.ds(r, bt, stride=rows), :] = cd * srow + k[:, r:r + 1] * v
        mu = jnp.mean(o, axis=-1, keepdims=True)
        var = jnp.mean(jnp.square(o - mu), axis=-1, keepdims=True)
        y = (o - mu) * lax.rsqrt(var + EPS) * gr_ref[:, h * RET_DV:(h + 1) * RET_DV]
        rg = rg_ref[:, h * RET_DV:(h + 1) * RET_DV]
        o_ref[:, h * RET_DV:(h + 1) * RET_DV] = rg * _sigmoid(rg) * y


def _ret_sample(rq, rk, rv, rg, g_ret, s0, bt):
    t = rq.shape[0]
    rows = RET_HEADS * RET_DK
    _, q_dec, _, chunk_dec = _ret_tables(1)
    qdec = jnp.repeat(q_dec, RET_DK, axis=1).reshape(1, RET_QK_W)
    cdec = jnp.repeat(chunk_dec, RET_DV).reshape(1, RET_V_W)
    tok = lambda w: pl.BlockSpec((bt, w), lambda i: (i, 0))
    const = lambda w: pl.BlockSpec((1, w), lambda i: (0, 0))
    st = pl.BlockSpec((bt * rows, RET_DV), lambda i: (i, 0))
    o, s1 = pl.pallas_call(
        functools.partial(_ret_sample_kernel, bt=bt),
        grid=(t // bt,),
        in_specs=[tok(RET_QK_W), tok(RET_QK_W), tok(RET_V_W), tok(RET_V_W), const(RET_V_W),
                  const(RET_QK_W), const(RET_V_W), st],
        out_specs=[tok(RET_V_W), st],
        out_shape=[jax.ShapeDtypeStruct((t, RET_V_W), F32), jax.ShapeDtypeStruct((t * rows, RET_DV), F32)],
        compiler_params=_params(("arbitrary",)),
        name="ret_sample",
    )(rq, rk, rv, rg, g_ret.reshape(1, RET_V_W), qdec, cdec, s0.reshape(t * rows, RET_DV))
    return o, s1.reshape(t, RET_HEADS, RET_DK, RET_DV)


def _xattn_prompt_kernel(q_ref, mk_ref, mv_ref, o_ref):
    scale = XA_HD ** -0.5
    for h in range(XA_HEADS):
        sl = slice(h * XA_HD, (h + 1) * XA_HD)
        s = _dot_nt(q_ref[0, :, sl].astype(BF16), mk_ref[0, :, sl].astype(BF16)) * scale
        p = jnp.exp(s - jnp.max(s, axis=1, keepdims=True))
        p = p / jnp.sum(p, axis=1, keepdims=True)
        o_ref[0, :, sl] = _dot(p.astype(BF16), mv_ref[0, :, sl].astype(BF16))


def _xattn_prompt(xq, mk, mv, t):
    b, s, _ = xq.shape
    m = mk.shape[1]
    return pl.pallas_call(
        _xattn_prompt_kernel,
        grid=(b, s // t),
        in_specs=[pl.BlockSpec((1, t, XA_W), lambda bi, i: (bi, i, 0)),
                  pl.BlockSpec((1, m, XA_W), lambda bi, i: (bi, 0, 0)),
                  pl.BlockSpec((1, m, XA_W), lambda bi, i: (bi, 0, 0))],
        out_specs=pl.BlockSpec((1, t, XA_W), lambda bi, i: (bi, i, 0)),
        out_shape=jax.ShapeDtypeStruct((b, s, XA_W), F32),
        compiler_params=_params(("arbitrary", "arbitrary")),
        name="xattn_prompt",
    )(xq, mk, mv)


_SUBLANES = 8


def _xattn_sample_kernel(q_ref, mk_ref, mv_ref, o_ref):
    tb, rows, _ = mk_ref.shape
    rep = _SUBLANES // XA_HEADS
    groups = rows // _SUBLANES

    def fold(x, op):
        out = x[:XA_HEADS]
        for r in range(1, rep):
            out = op(out, x[r * XA_HEADS:(r + 1) * XA_HEADS])
        return out

    for b in range(tb):
        q = q_ref[b]
        qt = jnp.concatenate([q] * rep, axis=0)[None]
        mk = mk_ref[b].reshape(groups, _SUBLANES, XA_HD)
        mv = mv_ref[b].reshape(groups, _SUBLANES, XA_HD)
        s = jnp.sum(mk * qt, axis=2, keepdims=True) * (XA_HD ** -0.5)
        m = fold(jnp.max(s, axis=0), jnp.maximum)
        p = jnp.exp(s - jnp.concatenate([m] * rep, axis=0)[None])
        l = fold(jnp.sum(p, axis=0), jnp.add)
        acc = fold(jnp.sum(p * mv, axis=0), jnp.add)
        o_ref[b] = acc / l


def _xattn_sample(xq, mk, mv, tb):
    t, rows, _ = mk.shape
    return pl.pallas_call(
        _xattn_sample_kernel,
        grid=(t // tb,),
        in_specs=[pl.BlockSpec((tb, XA_HEADS, XA_HD), lambda i: (i, 0, 0)),
                  pl.BlockSpec((tb, rows, XA_HD), lambda i: (i, 0, 0)),
                  pl.BlockSpec((tb, rows, XA_HD), lambda i: (i, 0, 0))],
        out_specs=pl.BlockSpec((tb, XA_HEADS, XA_HD), lambda i: (i, 0, 0)),
        out_shape=jax.ShapeDtypeStruct((t, XA_HEADS, XA_HD), F32),
        compiler_params=_params(("arbitrary",)),
        name="xattn_sample",
    )(xq, mk, mv)


_DECODE_PAGES = 16
_DECODE_SLOTS = 4


def _suffix_sum_lanes(x):
    n = x.shape[1]
    lane = lax.broadcasted_iota(jnp.int32, x.shape, 1)
    sh = 1
    while sh < n:
        x = x + jnp.where(lane < n - sh, pltpu.roll(x, n - sh, 1), 0.0)
        sh *= 2
    return x


def _fox_sample_kernel(pt_ref, q_ref, cn_ref, kn_ref, vn_ref, ck_hbm, clf_hbm, cv_hbm, o_ref,
                       buf_ref, lfbuf_ref, sem, lfsem, qb_ref, part_ref, p_ref, s_ref, m_ref, sn_ref, cnb_ref,
                       carry_ref, acc_ref, l_ref, *, pps, n_steps, n_pages, n_tok, n_slots):
    hrow = lax.broadcasted_iota(jnp.int32, (FOX_HEADS, FOX_W), 0)
    hlane = lax.broadcasted_iota(jnp.int32, (FOX_HEADS, FOX_W), 1)
    head_mask = (hlane >= hrow * FOX_HD) & (hlane < (hrow + 1) * FOX_HD)

    def rows_to_col(row_vals):
        r8 = lax.broadcasted_iota(jnp.int32, (FOX_HEADS, FOX_HEADS), 0)
        c8 = lax.broadcasted_iota(jnp.int32, (FOX_HEADS, FOX_HEADS), 1)
        b = jnp.broadcast_to(row_vals, (FOX_HEADS, FOX_HEADS))
        return jnp.sum(jnp.where(r8 == c8, b, 0.0), axis=1, keepdims=True)

    def col_to_head_row(col_vals):
        b = jnp.broadcast_to(col_vals, (FOX_HEADS, FOX_W))
        return jnp.sum(jnp.where(head_mask, b, 0.0), axis=0, keepdims=True)

    def unrolled(n, body, init=0):
        return lax.fori_loop(0, n, body, init, unroll=True)

    def static_when(cond):
        def run(f):
            if cond:
                f()
        return run

    def page_copies(keys, b, j, slot, i):
        page = 0 if b is None else pt_ref[b, n_pages - 1 - (j * pps + i)]
        if keys:
            return (pltpu.make_async_copy(ck_hbm.at[page], buf_ref.at[slot, i], sem.at[slot]),
                    pltpu.make_async_copy(clf_hbm.at[page], lfbuf_ref.at[slot, i], lfsem.at[slot]))
        return (pltpu.make_async_copy(cv_hbm.at[page], buf_ref.at[slot, i], sem.at[slot]),)

    def start_chunk(keys, b, j, slot):
        def body(i, c):
            for cp in page_copies(keys, b, j, slot, i):
                cp.start()
            return c
        unrolled(pps, body)

    def wait_chunk(keys, slot):
        def body(i, c):
            for cp in page_copies(keys, None, 0, slot, i):
                cp.wait()
            return c
        unrolled(pps, body)

    def key_chunk(b, j, slot):
        @static_when(j == 0)
        def _():
            q = q_ref[pl.ds(b, 1), :]
            qb_ref[...] = jnp.transpose(jnp.broadcast_to(q, (LANES, FOX_W)))
            cnb_ref[...] = jnp.broadcast_to(rows_to_col(cn_ref[pl.ds(b, 1), :]), cnb_ref.shape)
            carry_ref[...] = jnp.zeros_like(carry_ref)
            m_ref[...] = jnp.full_like(m_ref, NEG_INF)

        def head(h, c):
            qbh = qb_ref[pl.ds(pl.multiple_of(h * FOX_HD, FOX_HD), FOX_HD), :]

            def page_part(i, c2):
                prod = buf_ref[slot, i, h] * qbh
                part_ref[i, h] = jnp.sum(prod.reshape(FOX_HD // 8, 8, LANES), axis=0)
                return c2
            return unrolled(pps, page_part, c)
        unrolled(FOX_HEADS, head)

        cnb = cnb_ref[...]

        def page_scores(i, mc):
            m, carry = mc
            lf = lfbuf_ref[slot, i]
            suf = _suffix_sum_lanes(lf)
            s = jnp.sum(part_ref[i], axis=1) + cnb + ((suf - lf) + carry)
            s_ref[j * pps + i] = s
            return jnp.maximum(m, s), carry + suf[:, 0:1]
        m, carry = unrolled(pps, page_scores, (m_ref[...], carry_ref[...]))
        carry_ref[...] = carry
        m_ref[...] = m

        @static_when(j == n_steps - 1)
        def _():
            q8 = jnp.where(head_mask, jnp.broadcast_to(q_ref[pl.ds(b, 1), :], (FOX_HEADS, FOX_W)), 0.0)
            sn = jnp.sum(q8 * kn_ref[pl.ds(b, 1), :], axis=1, keepdims=True)
            mf = jnp.maximum(jnp.max(m, axis=1, keepdims=True), sn)
            sn_ref[...] = jnp.broadcast_to(sn, sn_ref.shape)
            m_ref[...] = jnp.broadcast_to(mf, m_ref.shape)

    def value_chunk(b, j, slot):
        @static_when(j == 0)
        def _():
            acc_ref[...] = jnp.zeros_like(acc_ref)
            l_ref[...] = jnp.zeros_like(l_ref)

        m = m_ref[...]

        def page_probs(i, l):
            p = jnp.exp(s_ref[j * pps + i] - m)
            p_ref[i] = p
            return l + p
        l = unrolled(pps, page_probs, l_ref[...])
        l_ref[...] = l

        def head(h, c):
            def page_acc(i, a):
                return a + p_ref[i, pl.ds(h, 1), :] * buf_ref[slot, i, h]
            acc_ref[h] = unrolled(pps, page_acc, acc_ref[h])
            return c
        unrolled(FOX_HEADS, head)

        @static_when(j == n_steps - 1)
        def _():
            pn = jnp.exp(sn_ref[:, 0:1] - m[:, 0:1])
            lsum = jnp.sum(l, axis=1, keepdims=True) + pn
            acc_t = jnp.transpose(acc_ref[...].reshape(FOX_W, LANES))
            acc = jnp.sum(acc_t, axis=0, keepdims=True) + col_to_head_row(pn) * vn_ref[pl.ds(b, 1), :]
            o_ref[pl.ds(b, 1), :] = acc / col_to_head_row(lsum)

    n_chunks = 2 * n_steps
    ahead = n_slots - 1

    def start_ahead(b, r):
        slot = r % n_slots
        if r < n_chunks:
            start_chunk(r < n_steps, b, r % n_steps, slot)
        else:
            @pl.when(b + 1 < n_tok)
            def _():
                start_chunk(True, b + 1, r - n_chunks, slot)

    for r in range(ahead):
        start_chunk(True, 0, r, r % n_slots)

    def token(b, carry):
        for r in range(n_chunks):
            slot = r % n_slots
            wait_chunk(r < n_steps, slot)
            start_ahead(b, r + ahead)
            if r < n_steps:
                key_chunk(b, r, slot)
            else:
                value_chunk(b, r - n_steps, slot)
        return carry

    lax.fori_loop(0, n_tok, token, 0)


def _fox_sample(page_table, fq, logf, fk, fv, ckt, cvt, clft, pps):
    t, n_pages = page_table.shape
    page = ckt.shape[3]
    assert page == LANES
    n_steps = n_pages // pps
    n_slots = _DECODE_SLOTS if (2 * n_steps) % _DECODE_SLOTS == 0 and _DECODE_SLOTS - 1 <= n_steps else 2
    full = lambda w: pl.BlockSpec((t, w), lambda i, pt: (0, 0))
    hbm = pl.BlockSpec(memory_space=pl.ANY)
    hb = (FOX_HEADS, LANES)
    grid_spec = pltpu.PrefetchScalarGridSpec(
        num_scalar_prefetch=1,
        grid=(1,),
        in_specs=[full(FOX_W), full(FOX_HEADS), full(FOX_W), full(FOX_W), hbm, hbm, hbm],
        out_specs=full(FOX_W),
        scratch_shapes=[pltpu.VMEM((n_slots, pps, FOX_HEADS, FOX_HD, LANES), F32),
                        pltpu.VMEM((n_slots, pps) + hb, F32),
                        pltpu.SemaphoreType.DMA((n_slots,)), pltpu.SemaphoreType.DMA((n_slots,)),
                        pltpu.VMEM((FOX_W, LANES), F32),
                        pltpu.VMEM((pps, FOX_HEADS, 8, LANES), F32),
                        pltpu.VMEM((pps,) + hb, F32),
                        pltpu.VMEM((n_pages,) + hb, F32),
                        pltpu.VMEM(hb, F32), pltpu.VMEM(hb, F32), pltpu.VMEM(hb, F32), pltpu.VMEM(hb, F32),
                        pltpu.VMEM((FOX_HEADS, FOX_HD, LANES), F32), pltpu.VMEM(hb, F32)],
    )
    return pl.pallas_call(
        functools.partial(_fox_sample_kernel, pps=pps, n_steps=n_steps, n_pages=n_pages, n_tok=t,
                          n_slots=n_slots),
        grid_spec=grid_spec,
        out_shape=jax.ShapeDtypeStruct((t, FOX_W), F32),
        compiler_params=_params(("arbitrary",)),
        name="fox_sample",
    )(page_table, fq, logf, fk, fv, ckt, clft, cvt)


def _memkv_kernel(x_ref, g_ref, w_ref, mk_ref, mv_ref):
    hn = _rms(x_ref[...], g_ref[...]).astype(BF16)
    mk_ref[...] = _dot(hn, w_ref[:, :XA_W])
    mv_ref[...] = _dot(hn, w_ref[:, XA_W:])


def _memkv(mem, g_mem, w_kv, tm):
    rows = mem.shape[0]
    return pl.pallas_call(
        _memkv_kernel,
        grid=(rows // tm,),
        in_specs=[pl.BlockSpec((tm, D_MODEL), lambda i: (i, 0)),
                  pl.BlockSpec((1, D_MODEL), lambda i: (0, 0)),
                  pl.BlockSpec((D_MODEL, 2 * XA_W), lambda i: (0, 0))],
        out_specs=[pl.BlockSpec((tm, XA_W), lambda i: (i, 0))] * 2,
        out_shape=[jax.ShapeDtypeStruct((rows, XA_W), F32)] * 2,
        compiler_params=_params(("arbitrary",)),
        name="memkv",
    )(mem, g_mem.reshape(1, D_MODEL), w_kv)


_MXU_DIM = 256
_FF_CHUNK = 4 * _MXU_DIM
_FF_CHUNKS = tuple((c, min(_FF_CHUNK, D_FF - c)) for c in range(0, D_FF, _FF_CHUNK))


def _post_kernel(x_ref, gates_ref, of_ref, or_ref, ox_ref, wf_ref, wr_ref, wx_ref, wo_ref, gffn_ref,
                 wgu_ref, wd_ref, gfin_ref, y_ref):
    mix = None
    for idx, (o_ref, w_ref) in enumerate(((of_ref, wf_ref), (or_ref, wr_ref), (ox_ref, wx_ref))):
        gate = _sigmoid(gates_ref[:, idx * D_MODEL:(idx + 1) * D_MODEL])
        term = gate * _dot(o_ref[...].astype(BF16), w_ref[...])
        mix = term if mix is None else mix + term
    x = x_ref[...] + _dot(mix.astype(BF16), wo_ref[...])
    hb = _rms(x, gffn_ref[...]).astype(BF16)
    ffn = jnp.zeros_like(x)
    for c, w in _FF_CHUNKS:
        u_gate = _dot(hb, wgu_ref[:, c:c + w])
        u_up = _dot(hb, wgu_ref[:, D_FF + c:D_FF + c + w])
        act = (u_gate * _sigmoid(u_gate) * u_up).astype(BF16)
        ffn = ffn + _dot(act, wd_ref[c:c + w, :])
    y_ref[...] = _rms(x + ffn, gfin_ref[...])


def _post(x, gates, o_fox, o_ret, o_xa, wf, wr, wx, wo, g_ffn, wgu, wd, g_final, tm):
    rows = x.shape[0]
    row = lambda w: pl.BlockSpec((tm, w), lambda i: (i, 0))
    const = lambda a, b: pl.BlockSpec((a, b), lambda i: (0, 0))
    return pl.pallas_call(
        _post_kernel,
        grid=(rows // tm,),
        in_specs=[row(D_MODEL), row(3 * D_MODEL), row(FOX_W), row(RET_V_W), row(XA_W),
                  const(FOX_W, D_MODEL), const(RET_V_W, D_MODEL), const(XA_W, D_MODEL),
                  const(D_MODEL, D_MODEL), const(1, D_MODEL), const(D_MODEL, 2 * D_FF),
                  const(D_FF, D_MODEL), const(1, D_MODEL)],
        out_specs=row(D_MODEL),
        out_shape=jax.ShapeDtypeStruct((rows, D_MODEL), F32),
        compiler_params=_params(("arbitrary",)),
        name="post",
    )(x, gates, o_fox, o_ret, o_xa, wf, wr, wx, wo, g_ffn.reshape(1, D_MODEL), wgu, wd,
      g_final.reshape(1, D_MODEL))


def _rotary_tables(pos):
    half = RET_DK // 2
    inv = ROPE_BASE ** (-jnp.arange(half, dtype=F32) / half)
    ang = pos.astype(F32)[:, None] * inv[None, :]
    cos, sin = jnp.cos(ang), jnp.sin(ang)
    zero = jnp.zeros_like(sin)
    tile = lambda a, b: jnp.tile(jnp.concatenate([a, b], axis=1), (1, RET_HEADS))
    return tile(cos, cos), tile(-sin, zero), tile(zero, sin)


def _pick_tile(n, pref):
    t = min(n, pref)
    while n % t:
        t //= 2
    return t


def kernel(x_prompt, x_sample, mem_prompt, cache_fox_k, cache_fox_v, cache_fox_logf, state_ret, cache_mem_k, cache_mem_v, page_table, g_attn, w_in, b_f, g_ret, w_br_fox, w_br_ret, w_br_xa, w_o, g_ffn, w_gu, w_down, g_mem, w_mem_kv, g_final):
    bp, sp, _ = x_prompt.shape
    bs, ts, _ = x_sample.shape
    depth = w_in.shape[0]
    assert depth == 1 and ts == 1
    n_pages, page = page_table.shape[1], cache_fox_k.shape[2]
    past = n_pages * page
    n_mem = mem_prompt.shape[1]
    l = 0

    flog0 = 3 * FOX_W
    w_l = w_in[l]
    w_main = jnp.concatenate([w_l[:, :flog0], w_l[:, flog0 + FOX_HEADS:]], axis=1).astype(BF16)
    w_kvt = w_l[:, FOX_W:flog0].T.astype(BF16)
    w_flt = w_l[:, flog0:flog0 + FOX_HEADS].T.astype(BF16)
    wf, wr, wx = w_br_fox[l].astype(BF16), w_br_ret[l].astype(BF16), w_br_xa[l].astype(BF16)
    wo, wgu, wd = w_o[l].astype(BF16), w_gu[l].astype(BF16), w_down[l].astype(BF16)
    w_kv = w_mem_kv[l].astype(BF16)

    def token_major(a_t, n_heads, head_w):
        b, _, s = a_t.shape
        return a_t.reshape(b, n_heads, head_w, s).transpose(0, 3, 1, 2)[None]

    mk_p, mv_p = _memkv(mem_prompt.reshape(bp * n_mem, D_MODEL), g_mem[l], w_kv, _pick_tile(bp * n_mem, 256))
    mk_p = mk_p.reshape(bp, n_mem, XA_W)
    mv_p = mv_p.reshape(bp, n_mem, XA_W)
    tm = _pick_tile(sp, 256)
    cos, s1, s2 = _rotary_tables(jnp.arange(sp))
    (gates, fqb, fkt, fvt, fktb, fvb, lft, ct, rq, rk, rv, rg, xq) = _inproj(
        x_prompt, g_attn[l], w_main, w_kvt, w_flt, b_f[l], cos, s1, s2, tm, False)
    o_fox = _fox_prompt(fqb, fktb, fvb, ct, _pick_tile(sp // 2, 512))
    o_ret, st_p = _ret_prompt(rq, rk, rv, rg, g_ret[l])
    o_xa = _xattn_prompt(xq, mk_p, mv_p, _pick_tile(sp, 1024))
    rows = bp * sp
    y_p = _post(x_prompt.reshape(rows, D_MODEL), gates.reshape(rows, 3 * D_MODEL), o_fox.reshape(rows, FOX_W),
                o_ret.reshape(rows, RET_V_W), o_xa.reshape(rows, XA_W), wf, wr, wx, wo, g_ffn[l], wgu, wd,
                g_final, _pick_tile(rows, 256)).reshape(bp, sp, D_MODEL)

    cos_s, s1_s, s2_s = _rotary_tables(jnp.full((bs,), past, jnp.int32))
    (gates_s, fqb_s, fkt_s, fvt_s, _, fvb_s, lft_s, _, rq_s, rk_s, rv_s, rg_s, xq_s, fkb_s) = _inproj(
        x_sample.reshape(1, bs, D_MODEL), g_attn[l], w_main, w_kvt, w_flt, b_f[l], cos_s, s1_s, s2_s, bs, True)
    two = lambda a: a.reshape(bs, a.shape[-1])
    ckt = cache_fox_k[l].transpose(0, 2, 3, 1)
    cvt = cache_fox_v[l].transpose(0, 2, 3, 1)
    clft = cache_fox_logf[l].transpose(0, 2, 1)
    o_fox_s = _fox_sample(page_table, two(fqb_s).astype(F32), lft_s[0].T, two(fkb_s).astype(F32),
                          two(fvb_s).astype(F32), ckt, cvt, clft, _pick_tile(n_pages, _DECODE_PAGES))
    o_ret_s, st_s = _ret_sample(two(rq_s), two(rk_s), two(rv_s), two(rg_s), g_ret[l], state_ret[l],
                                _pick_tile(bs, 32))
    o_xa_s = _xattn_sample(two(xq_s).reshape(bs, XA_HEADS, XA_HD),
                           cache_mem_k[l].reshape(bs, n_mem * XA_HEADS, XA_HD),
                           cache_mem_v[l].reshape(bs, n_mem * XA_HEADS, XA_HD),
                           _pick_tile(bs, 8)).reshape(bs, XA_W)
    y_s = _post(x_sample.reshape(bs, D_MODEL), two(gates_s), o_fox_s, o_ret_s, o_xa_s, wf, wr, wx, wo,
                g_ffn[l], wgu, wd, g_final, bs).reshape(bs, 1, D_MODEL)

    stack = lambda a, shape: a.reshape((1,) + shape)
    sample_major = lambda a_t, n_heads, head_w: (
        a_t.reshape(n_heads, head_w, bs).transpose(2, 0, 1).reshape(1, bs, 1, n_heads, head_w))
    return (y_p, y_s,
            token_major(fkt, FOX_HEADS, FOX_HD), token_major(fvt, FOX_HEADS, FOX_HD),
            lft.transpose(0, 2, 1)[None], stack(st_p, (bp, RET_HEADS, RET_DK, RET_DV)),
            stack(mk_p, (bp, n_mem, XA_HEADS, XA_HD)), stack(mv_p, (bp, n_mem, XA_HEADS, XA_HD)),
            sample_major(fkt_s, FOX_HEADS, FOX_HD), sample_major(fvt_s, FOX_HEADS, FOX_HD),
            lft_s[0].T.reshape(1, bs, 1, FOX_HEADS), stack(st_s, (bs, RET_HEADS, RET_DK, RET_DV)))
```

```python
import functools

import jax
import jax.numpy as jnp
from jax import lax
from jax.experimental import pallas as pl
from jax.experimental.pallas import tpu as pltpu

D_MODEL = 1024
FOX_HEADS = 8
FOX_HD = 64
RET_HEADS = 4
RET_DK = 64
RET_DV = 128
XA_HEADS = 4
XA_HD = 128
FOX_W = FOX_HEADS * FOX_HD
RET_QK_W = RET_HEADS * RET_DK
RET_V_W = RET_HEADS * RET_DV
XA_W = XA_HEADS * XA_HD
D_FF = 2816
RET_CHUNK = 128
ROPE_BASE = 10000.0
EPS = 1e-6
NEG_INF = -1e30

LANES = 128
VMEM_LIMIT = 56 * 1024 * 1024

F32 = jnp.float32
BF16 = jnp.bfloat16

_C_FQ, _C_FK, _C_FV = 0, 512, 1024
_C_RQK, _C_RV, _C_RG, _C_XQ, _C_GATES = 1536, 2048, 2560, 3072, 3584
_W_MAIN = 6656


def _params(sem):
    return pltpu.CompilerParams(dimension_semantics=sem, vmem_limit_bytes=VMEM_LIMIT)


def _dot(a, b):
    return jnp.dot(a, b, preferred_element_type=F32)


def _dot_nt(a, b):
    return lax.dot_general(a, b, (((1,), (1,)), ((), ())), preferred_element_type=F32)


def _dot_tn(a, b):
    return lax.dot_general(a, b, (((0,), (0,)), ((), ())), preferred_element_type=F32)


def _rms(x, g):
    return x * lax.rsqrt(jnp.mean(x * x, axis=-1, keepdims=True) + EPS) * g


def _log_sigmoid(x):
    return -(jnp.maximum(-x, 0.0) + jnp.log1p(jnp.exp(-jnp.abs(x))))


def _sigmoid(x):
    return 1.0 / (1.0 + jnp.exp(-x))


def _split3(x):
    hi = x.astype(BF16)
    r1 = x - hi.astype(F32)
    mid = r1.astype(BF16)
    lo = (r1 - mid.astype(F32)).astype(BF16)
    return hi, mid, lo


def _inproj_kernel(x_ref, g_ref, wm_ref, wkvt_ref, wflt_ref, bft_ref, cos_ref, s1_ref, s2_ref,
                   gates_ref, fq_ref, fkt_ref, fvt_ref, fktb_ref, fvb_ref, lft_ref, ct_ref,
                   rq_ref, rk_ref, rv_ref, rg_ref, xq_ref, *tail, tm):
    carry_ref = tail[-1]
    j = pl.program_id(1)
    hn = _rms(x_ref[0], g_ref[...]).astype(BF16)

    def proj(c0, width):
        return _dot(hn, wm_ref[:, c0:c0 + width])

    fq_ref[0] = (proj(_C_FQ, FOX_W) * (FOX_HD ** -0.5)).astype(BF16)
    if len(tail) == 2:
        tail[0][0] = proj(_C_FK, FOX_W).astype(BF16)
    fkt = _dot_nt(wkvt_ref[:FOX_W, :], hn)
    fkt_ref[0] = fkt
    fktb_ref[0] = fkt.astype(BF16)
    fvt = _dot_nt(wkvt_ref[FOX_W:, :], hn)
    fvt_ref[0] = fvt
    fvb_ref[0] = fvt.T.astype(BF16)

    rqk = proj(_C_RQK, 2 * RET_QK_W)
    cos, s1, s2 = cos_ref[...], s1_ref[...], s2_ref[...]
    for ref, off in ((rq_ref, 0), (rk_ref, RET_QK_W)):
        r = rqk[:, off:off + RET_QK_W]
        ref[0] = (r * cos + pltpu.roll(r, RET_QK_W - RET_DK // 2, 1) * s1
                  + pltpu.roll(r, RET_DK // 2, 1) * s2)
    rv_ref[0] = proj(_C_RV, RET_V_W)
    rg_ref[0] = proj(_C_RG, RET_V_W)
    xq_ref[0] = proj(_C_XQ, XA_W)
    for c in range(0, 3 * D_MODEL, D_MODEL):
        gates_ref[0, :, c:c + D_MODEL] = proj(_C_GATES + c, D_MODEL)

    lft = _log_sigmoid(_dot_nt(wflt_ref[...], hn) + bft_ref[...])
    lft_ref[0] = lft

    @pl.when(j == 0)
    def _():
        carry_ref[...] = jnp.zeros_like(carry_ref)

    row = lax.broadcasted_iota(jnp.int32, (tm, tm), 0)
    col = lax.broadcasted_iota(jnp.int32, (tm, tm), 1)
    tri = (row <= col).astype(BF16)
    hi, mid, lo = _split3(lft)
    ct = _dot(hi, tri) + _dot(mid, tri) + _dot(lo, tri) + carry_ref[:, 0:1]
    ct_ref[0] = ct
    carry_ref[...] = jnp.broadcast_to(ct[:, tm - 1:tm], carry_ref.shape)


def _inproj(x, g_attn, w_main, w_kvt, w_flt, b_f, cos, s1, s2, tm, token_major_k):
    b, s, _ = x.shape
    grid = (b, s // tm)
    tok = lambda w: pl.BlockSpec((1, tm, w), lambda i, j: (i, j, 0))
    feat = lambda h: pl.BlockSpec((1, h, tm), lambda i, j: (i, 0, j))
    const = lambda shape: pl.BlockSpec(shape, lambda i, j: (0,) * len(shape))
    tab = pl.BlockSpec((tm, RET_QK_W), lambda i, j: (j, 0))
    outs = [
        (tok, 3 * D_MODEL, F32), (tok, FOX_W, BF16),
        (feat, FOX_W, F32), (feat, FOX_W, F32), (feat, FOX_W, BF16),
        (tok, FOX_W, BF16),
        (feat, FOX_HEADS, F32), (feat, FOX_HEADS, F32),
        (tok, RET_QK_W, F32), (tok, RET_QK_W, F32), (tok, RET_V_W, F32), (tok, RET_V_W, F32), (tok, XA_W, F32),
    ] + ([(tok, FOX_W, BF16)] if token_major_k else [])
    shape = lambda kind, w: (b, s, w) if kind is tok else (b, w, s)
    return pl.pallas_call(
        functools.partial(_inproj_kernel, tm=tm),
        grid=grid,
        in_specs=[tok(D_MODEL), const((1, D_MODEL)), const((D_MODEL, _W_MAIN)), const((2 * FOX_W, D_MODEL)),
                  const((FOX_HEADS, D_MODEL)), const((FOX_HEADS, 1)), tab, tab, tab],
        out_specs=[kind(w) for kind, w, _ in outs],
        out_shape=[jax.ShapeDtypeStruct(shape(kind, w), dt) for kind, w, dt in outs],
        scratch_shapes=[pltpu.VMEM((FOX_HEADS, LANES), F32)],
        compiler_params=_params(("arbitrary", "arbitrary")),
        name="inproj",
    )(x, g_attn.reshape(1, D_MODEL), w_main, w_kvt, w_flt, b_f.reshape(FOX_HEADS, 1), cos, s1, s2)


def _lane_blocks(x, op):
    out = x[:, :LANES]
    for c in range(LANES, x.shape[1], LANES):
        out = op(out, x[:, c:c + LANES])
    return out


_LOG2E = 1.4426950408889634


def _fox_prompt_kernel(q_ref, kt_ref, v_ref, ct_ref, o_ref, s_ref, mx_ref, l_ref, acc_ref, *, t):
    i = pl.program_id(2)
    lane = lax.broadcasted_iota(jnp.int32, (t, LANES), 1)
    row = lax.broadcasted_iota(jnp.int32, (t, t), 0)
    col = lax.broadcasted_iota(jnp.int32, (t, t), 1)
    chains = [(hh, half) for hh in range(2) for half in range(2)]
    qm = []
    for hh, half in chains:
        q = q_ref[0, half * t:(half + 1) * t, :]
        head_lanes = (lane >= hh * FOX_HD) & (lane < (hh + 1) * FOX_HD)
        qm.append(jnp.where(head_lanes, q, jnp.zeros_like(q)))

    def score_tile(c, j, diag):
        hh = chains[c][0]
        off = pl.multiple_of(j * t, t)
        s = (_dot(qm[c], kt_ref[0, :, pl.ds(off, t)]) - ct_ref[0, 0, hh:hh + 1, pl.ds(off, t)]) * _LOG2E
        if diag:
            s = jnp.where(col <= row, s, NEG_INF)
        s_ref[c, j] = s
        mx_ref[c] = jnp.maximum(mx_ref[c], _lane_blocks(s, jnp.maximum))

    tail = {0: ((0, True),), 1: ((0, False), (1, True))}

    mx_ref[...] = jnp.full_like(mx_ref, NEG_INF)

    def pass1(j, carry):
        for c in range(4):
            score_tile(c, j, False)
        return carry

    lax.fori_loop(0, 2 * i, pass1, 0)
    for c, (_, half) in enumerate(chains):
        for dj, diag in tail[half]:
            score_tile(c, 2 * i + dj, diag)

    m = [jnp.max(mx_ref[c], axis=1, keepdims=True) for c in range(4)]
    l_ref[...] = jnp.zeros_like(l_ref)
    acc_ref[...] = jnp.zeros_like(acc_ref)

    def value_tile(c, j):
        off = pl.multiple_of(j * t, t)
        p = jnp.exp2(s_ref[c, j] - m[c])
        l_ref[c] += _lane_blocks(p, jnp.add)
        acc_ref[c] += _dot(p.astype(BF16), v_ref[0, pl.ds(off, t), :])

    def pass2(j, carry):
        for c in range(4):
            value_tile(c, j)
        return carry

    lax.fori_loop(0, 2 * i, pass2, 0)
    for c, (_, half) in enumerate(chains):
        for dj, _ in tail[half]:
            value_tile(c, 2 * i + dj)

    out = [acc_ref[c] / jnp.sum(l_ref[c], axis=1, keepdims=True) for c in range(4)]
    for half in range(2):
        o_ref[0, half * t:(half + 1) * t, :] = jnp.where(lane < FOX_HD, out[half], out[2 + half])


def _fox_prompt(fqb, fktb, fvb, ct, t):
    b, s, _ = fqb.shape
    pairs = FOX_HEADS // 2
    ct4 = ct.reshape(b, pairs, 2, s)
    return pl.pallas_call(
        functools.partial(_fox_prompt_kernel, t=t),
        grid=(b, pairs, s // (2 * t)),
        in_specs=[pl.BlockSpec((1, 2 * t, LANES), lambda bi, pr, i: (bi, i, pr)),
                  pl.BlockSpec((1, LANES, s), lambda bi, pr, i: (bi, pr, 0)),
                  pl.BlockSpec((1, s, LANES), lambda bi, pr, i: (bi, 0, pr)),
                  pl.BlockSpec((1, 1, 2, s), lambda bi, pr, i: (bi, pr, 0, 0))],
        out_specs=pl.BlockSpec((1, 2 * t, LANES), lambda bi, pr, i: (bi, i, pr)),
        out_shape=jax.ShapeDtypeStruct((b, s, FOX_W), F32),
        scratch_shapes=[pltpu.VMEM((4, s // t, t, t), F32), pltpu.VMEM((4, t, LANES), F32),
                        pltpu.VMEM((4, t, LANES), F32), pltpu.VMEM((4, t, LANES), F32)],
        compiler_params=_params(("arbitrary", "arbitrary", "arbitrary")),
        name="fox_prompt",
    )(fqb, fktb, fvb, ct4)


def _ret_prompt_kernel(q_ref, k_ref, v_ref, rg_ref, gr_ref, dmask_ref, qdec_ref, kdec_ref, cdec_ref,
                       o_ref, st_ref):
    n = pl.program_id(1)

    @pl.when(n == 0)
    def _():
        st_ref[...] = jnp.zeros_like(st_ref)

    c = RET_CHUNK
    lane = lax.broadcasted_iota(jnp.int32, (c, LANES), 1)
    for bi in range(q_ref.shape[0]):
        for pr in range(RET_HEADS // 2):
            q = q_ref[bi, :, pr * LANES:(pr + 1) * LANES]
            k = k_ref[bi, :, pr * LANES:(pr + 1) * LANES] * (RET_DK ** -0.5)
            state = st_ref[bi, pr]
            qd = (q * qdec_ref[pr]).astype(BF16)
            kd = k * kdec_ref[pr]
            kb = k.astype(BF16)
            new_state = cdec_ref[pr] * state
            state_b = state.astype(BF16)
            for hh in range(2):
                vs = slice((2 * pr + hh) * RET_DV, (2 * pr + hh + 1) * RET_DV)
                head_lanes = (lane >= hh * RET_DK) & (lane < (hh + 1) * RET_DK)
                vb = v_ref[bi, :, vs].astype(BF16)
                qm = jnp.where(head_lanes, q, 0.0).astype(BF16)
                inner = _dot_nt(qm, kb) * dmask_ref[pr, hh]
                qdm = jnp.where(head_lanes, qd, jnp.zeros_like(qd))
                o = _dot(inner.astype(BF16), vb) + _dot(qdm, state_b)
                kdm = jnp.where(head_lanes, kd, 0.0).astype(BF16)
                new_state = new_state + _dot_tn(kdm, vb)
                mu = jnp.mean(o, axis=-1, keepdims=True)
                var = jnp.mean(jnp.square(o - mu), axis=-1, keepdims=True)
                y = (o - mu) * lax.rsqrt(var + EPS) * gr_ref[:, vs]
                rg = rg_ref[bi, :, vs]
                o_ref[bi, :, vs] = rg * _sigmoid(rg) * y
            st_ref[bi, pr] = new_state


def _ret_tables(length):
    h = RET_HEADS
    log_g = jnp.log(1.0 - 2.0 ** (-5.0 - jnp.arange(h, dtype=F32)))
    i = jnp.arange(length, dtype=F32)
    diff = i[:, None] - i[None, :]
    dmask = jnp.where(diff[None] >= 0, jnp.exp(jnp.maximum(diff, 0.0)[None] * log_g[:, None, None]), 0.0)
    q_dec = jnp.exp((i + 1.0)[:, None] * log_g[None, :])
    k_dec = jnp.exp((length - 1.0 - i)[:, None] * log_g[None, :])
    chunk_dec = jnp.exp(length * log_g)
    return dmask, q_dec, k_dec, chunk_dec


def _ret_prompt(rq, rk, rv, rg, g_ret):
    b, s, _ = rq.shape
    c = RET_CHUNK
    pairs = RET_HEADS // 2
    dmask, q_dec, k_dec, chunk_dec = _ret_tables(c)
    dmask = dmask.reshape(pairs, 2, c, c)
    qdec = jnp.repeat(q_dec, RET_DK, axis=1).reshape(c, pairs, LANES).transpose(1, 0, 2)
    kdec = jnp.repeat(k_dec, RET_DK, axis=1).reshape(c, pairs, LANES).transpose(1, 0, 2)
    cdec = jnp.repeat(chunk_dec, RET_DK).reshape(pairs, LANES, 1)
    bb = _pick_tile(b, 8)
    tok = lambda w: pl.BlockSpec((bb, c, w), lambda bi, n: (bi, n, 0))
    const = lambda shape: pl.BlockSpec(shape, lambda bi, n: (0,) * len(shape))
    o, st = pl.pallas_call(
        _ret_prompt_kernel,
        grid=(b // bb, s // c),
        in_specs=[tok(RET_QK_W), tok(RET_QK_W), tok(RET_V_W), tok(RET_V_W), const((1, RET_V_W)),
                  const((pairs, 2, c, c)), const((pairs, c, LANES)), const((pairs, c, LANES)),
                  const((pairs, LANES, 1))],
        out_specs=[tok(RET_V_W),
                   pl.BlockSpec((bb, pairs, 2 * RET_DK, RET_DV), lambda bi, n: (bi, 0, 0, 0))],
        out_shape=[jax.ShapeDtypeStruct((b, s, RET_V_W), F32),
                   jax.ShapeDtypeStruct((b, pairs, 2 * RET_DK, RET_DV), F32)],
        compiler_params=_params(("arbitrary", "arbitrary")),
        name="ret_prompt",
    )(rq, rk, rv, rg, g_ret.reshape(1, RET_V_W), dmask, qdec, kdec, cdec)
    return o, st.reshape(b, RET_HEADS, RET_DK, RET_DV)


def _ret_sample_kernel(q_ref, k_ref, v_ref, rg_ref, gr_ref, qdec_ref, cdec_ref, s0_ref, o_ref, s1_ref, *, bt):
    rows = RET_HEADS * RET_DK
    q = q_ref[...]
    k = k_ref[...] * (RET_DK ** -0.5)
    qg = q * qdec_ref[...]
    qk = q * k
    for h in range(RET_HEADS):
        v = v_ref[:, h * RET_DV:(h + 1) * RET_DV]
        cd = cdec_ref[:, h * RET_DV:(h + 1) * RET_DV]
        inner = jnp.sum(qk[:, h * RET_DK:(h + 1) * RET_DK], axis=1, keepdims=True)
        o = inner * v
        for d in range(RET_DK):
            r = h * RET_DK + d
            srow = s0_ref[pl.ds(r, bt, stride=rows), :]
            o = o + qg[:, r:r + 1] * srow
            s1_ref[pl.ds(r, bt, stride=rows), :] = cd * srow + k[:, r:r + 1] * v
        mu = jnp.mean(o, axis=-1, keepdims=True)
        var = jnp.mean(jnp.square(o - mu), axis=-1, keepdims=True)
        y = (o - mu) * lax.rsqrt(var + EPS) * gr_ref[:, h * RET_DV:(h + 1) * RET_DV]
        rg = rg_ref[:, h * RET_DV:(h + 1) * RET_DV]
        o_ref[:, h * RET_DV:(h + 1) * RET_DV] = rg * _sigmoid(rg) * y


def _ret_sample(rq, rk, rv, rg, g_ret, s0, bt):
    t = rq.shape[0]
    rows = RET_HEADS * RET_DK
    _, q_dec, _, chunk_dec = _ret_tables(1)
    qdec = jnp.repeat(q_dec, RET_DK, axis=1).reshape(1, RET_QK_W)
    cdec = jnp.repeat(chunk_dec, RET_DV).reshape(1, RET_V_W)
    tok = lambda w: pl.BlockSpec((bt, w), lambda i: (i, 0))
    const = lambda w: pl.BlockSpec((1, w), lambda i: (0, 0))
    st = pl.BlockSpec((bt * rows, RET_DV), lambda i: (i, 0))
    o, s1 = pl.pallas_call(
        functools.partial(_ret_sample_kernel, bt=bt),
        grid=(t // bt,),
        in_specs=[tok(RET_QK_W), tok(RET_QK_W), tok(RET_V_W), tok(RET_V_W), const(RET_V_W),
                  const(RET_QK_W), const(RET_V_W), st],
        out_specs=[tok(RET_V_W), st],
        out_shape=[jax.ShapeDtypeStruct((t, RET_V_W), F32), jax.ShapeDtypeStruct((t * rows, RET_DV), F32)],
        compiler_params=_params(("arbitrary",)),
        name="ret_sample",
    )(rq, rk, rv, rg, g_ret.reshape(1, RET_V_W), qdec, cdec, s0.reshape(t * rows, RET_DV))
    return o, s1.reshape(t, RET_HEADS, RET_DK, RET_DV)


def _xattn_prompt_kernel(q_ref, mk_ref, mv_ref, o_ref):
    scale = XA_HD ** -0.5
    for h in range(XA_HEADS):
        sl = slice(h * XA_HD, (h + 1) * XA_HD)
        s = _dot_nt(q_ref[0, :, sl].astype(BF16), mk_ref[0, :, sl].astype(BF16)) * scale
        p = jnp.exp(s - jnp.max(s, axis=1, keepdims=True))
        p = p / jnp.sum(p, axis=1, keepdims=True)
        o_ref[0, :, sl] = _dot(p.astype(BF16), mv_ref[0, :, sl].astype(BF16))


def _xattn_prompt(xq, mk, mv, t):
    b, s, _ = xq.shape
    m = mk.shape[1]
    return pl.pallas_call(
        _xattn_prompt_kernel,
        grid=(b, s // t),
        in_specs=[pl.BlockSpec((1, t, XA_W), lambda bi, i: (bi, i, 0)),
                  pl.BlockSpec((1, m, XA_W), lambda bi, i: (bi, 0, 0)),
                  pl.BlockSpec((1, m, XA_W), lambda bi, i: (bi, 0, 0))],
        out_specs=pl.BlockSpec((1, t, XA_W), lambda bi, i: (bi, i, 0)),
        out_shape=jax.ShapeDtypeStruct((b, s, XA_W), F32),
        compiler_params=_params(("arbitrary", "arbitrary")),
        name="xattn_prompt",
    )(xq, mk, mv)


_SUBLANES = 8


def _xattn_sample_kernel(q_ref, mk_ref, mv_ref, o_ref):
    tb, rows, _ = mk_ref.shape
    rep = _SUBLANES // XA_HEADS
    groups = rows // _SUBLANES

    def fold(x, op):
        out = x[:XA_HEADS]
        for r in range(1, rep):
            out = op(out, x[r * XA_HEADS:(r + 1) * XA_HEADS])
        return out

    for b in range(tb):
        q = q_ref[b]
        qt = jnp.concatenate([q] * rep, axis=0)[None]
        mk = mk_ref[b].reshape(groups, _SUBLANES, XA_HD)
        mv = mv_ref[b].reshape(groups, _SUBLANES, XA_HD)
        s = jnp.sum(mk * qt, axis=2, keepdims=True) * (XA_HD ** -0.5)
        m = fold(jnp.max(s, axis=0), jnp.maximum)
        p = jnp.exp(s - jnp.concatenate([m] * rep, axis=0)[None])
        l = fold(jnp.sum(p, axis=0), jnp.add)
        acc = fold(jnp.sum(p * mv, axis=0), jnp.add)
        o_ref[b] = acc / l


def _xattn_sample(xq, mk, mv, tb):
    t, rows, _ = mk.shape
    return pl.pallas_call(
        _xattn_sample_kernel,
        grid=(t // tb,),
        in_specs=[pl.BlockSpec((tb, XA_HEADS, XA_HD), lambda i: (i, 0, 0)),
                  pl.BlockSpec((tb, rows, XA_HD), lambda i: (i, 0, 0)),
                  pl.BlockSpec((tb, rows, XA_HD), lambda i: (i, 0, 0))],
        out_specs=pl.BlockSpec((tb, XA_HEADS, XA_HD), lambda i: (i, 0, 0)),
        out_shape=jax.ShapeDtypeStruct((t, XA_HEADS, XA_HD), F32),
        compiler_params=_params(("arbitrary",)),
        name="xattn_sample",
    )(xq, mk, mv)


_DECODE_PAGES = 16
_DECODE_SLOTS = 4


def _suffix_sum_lanes(x):
    n = x.shape[1]
    lane = lax.broadcasted_iota(jnp.int32, x.shape, 1)
    sh = 1
    while sh < n:
        x = x + jnp.where(lane < n - sh, pltpu.roll(x, n - sh, 1), 0.0)
        sh *= 2
    return x


def _fox_sample_kernel(pt_ref, q_ref, cn_ref, kn_ref, vn_ref, ck_hbm, clf_hbm, cv_hbm, o_ref,
                       buf_ref, lfbuf_ref, sem, lfsem, qb_ref, part_ref, p_ref, s_ref, m_ref, sn_ref, cnb_ref,
                       carry_ref, acc_ref, l_ref, *, pps, n_steps, n_pages, n_tok, n_slots):
    hrow = lax.broadcasted_iota(jnp.int32, (FOX_HEADS, FOX_W), 0)
    hlane = lax.broadcasted_iota(jnp.int32, (FOX_HEADS, FOX_W), 1)
    head_mask = (hlane >= hrow * FOX_HD) & (hlane < (hrow + 1) * FOX_HD)

    def rows_to_col(row_vals):
        r8 = lax.broadcasted_iota(jnp.int32, (FOX_HEADS, FOX_HEADS), 0)
        c8 = lax.broadcasted_iota(jnp.int32, (FOX_HEADS, FOX_HEADS), 1)
        b = jnp.broadcast_to(row_vals, (FOX_HEADS, FOX_HEADS))
        return jnp.sum(jnp.where(r8 == c8, b, 0.0), axis=1, keepdims=True)

    def col_to_head_row(col_vals):
        b = jnp.broadcast_to(col_vals, (FOX_HEADS, FOX_W))
        return jnp.sum(jnp.where(head_mask, b, 0.0), axis=0, keepdims=True)

    def unrolled(n, body, init=0):
        return lax.fori_loop(0, n, body, init, unroll=True)

    def static_when(cond):
        def run(f):
            if cond:
                f()
        return run

    def page_copies(keys, b, j, slot, i):
        page = 0 if b is None else pt_ref[b, n_pages - 1 - (j * pps + i)]
        if keys:
            return (pltpu.make_async_copy(ck_hbm.at[page], buf_ref.at[slot, i], sem.at[slot]),
                    pltpu.make_async_copy(clf_hbm.at[page], lfbuf_ref.at[slot, i], lfsem.at[slot]))
        return (pltpu.make_async_copy(cv_hbm.at[page], buf_ref.at[slot, i], sem.at[slot]),)

    def start_chunk(keys, b, j, slot):
        def body(i, c):
            for cp in page_copies(keys, b, j, slot, i):
                cp.start()
            return c
        unrolled(pps, body)

    def wait_chunk(keys, slot):
        def body(i, c):
            for cp in page_copies(keys, None, 0, slot, i):
                cp.wait()
            return c
        unrolled(pps, body)

    def key_chunk(b, j, slot):
        @static_when(j == 0)
        def _():
            q = q_ref[pl.ds(b, 1), :]
            qb_ref[...] = jnp.transpose(jnp.broadcast_to(q, (LANES, FOX_W)))
            cnb_ref[...] = jnp.broadcast_to(rows_to_col(cn_ref[pl.ds(b, 1), :]), cnb_ref.shape)
            carry_ref[...] = jnp.zeros_like(carry_ref)
            m_ref[...] = jnp.full_like(m_ref, NEG_INF)

        def head(h, c):
            qbh = qb_ref[pl.ds(pl.multiple_of(h * FOX_HD, FOX_HD), FOX_HD), :]

            def page_part(i, c2):
                prod = buf_ref[slot, i, h] * qbh
                part_ref[i, h] = jnp.sum(prod.reshape(FOX_HD // 8, 8, LANES), axis=0)
                return c2
            return unrolled(pps, page_part, c)
        unrolled(FOX_HEADS, head)

        cnb = cnb_ref[...]

        def page_scores(i, mc):
            m, carry = mc
            lf = lfbuf_ref[slot, i]
            suf = _suffix_sum_lanes(lf)
            s = jnp.sum(part_ref[i], axis=1) + cnb + ((suf - lf) + carry)
            s_ref[j * pps + i] = s
            return jnp.maximum(m, s), carry + suf[:, 0:1]
        m, carry = unrolled(pps, page_scores, (m_ref[...], carry_ref[...]))
        carry_ref[...] = carry
        m_ref[...] = m

        @static_when(j == n_steps - 1)
        def _():
            q8 = jnp.where(head_mask, jnp.broadcast_to(q_ref[pl.ds(b, 1), :], (FOX_HEADS, FOX_W)), 0.0)
            sn = jnp.sum(q8 * kn_ref[pl.ds(b, 1), :], axis=1, keepdims=True)
            mf = jnp.maximum(jnp.max(m, axis=1, keepdims=True), sn)
            sn_ref[...] = jnp.broadcast_to(sn, sn_ref.shape)
            m_ref[...] = jnp.broadcast_to(mf, m_ref.shape)

    def value_chunk(b, j, slot):
        @static_when(j == 0)
        def _():
            acc_ref[...] = jnp.zeros_like(acc_ref)
            l_ref[...] = jnp.zeros_like(l_ref)

        m = m_ref[...]

        def page_probs(i, l):
            p = jnp.exp(s_ref[j * pps + i] - m)
            p_ref[i] = p
            return l + p
        l = unrolled(pps, page_probs, l_ref[...])
        l_ref[...] = l

        def head(h, c):
            def page_acc(i, a):
                return a + p_ref[i, pl.ds(h, 1), :] * buf_ref[slot, i, h]
            acc_ref[h] = unrolled(pps, page_acc, acc_ref[h])
            return c
        unrolled(FOX_HEADS, head)

        @static_when(j == n_steps - 1)
        def _():
            pn = jnp.exp(sn_ref[:, 0:1] - m[:, 0:1])
            lsum = jnp.sum(l, axis=1, keepdims=True) + pn
            acc_t = jnp.transpose(acc_ref[...].reshape(FOX_W, LANES))
            acc = jnp.sum(acc_t, axis=0, keepdims=True) + col_to_head_row(pn) * vn_ref[pl.ds(b, 1), :]
            o_ref[pl.ds(b, 1), :] = acc / col_to_head_row(lsum)

    n_chunks = 2 * n_steps
    ahead = n_slots - 1

    def start_ahead(b, r):
        slot = r % n_slots
        if r < n_chunks:
            start_chunk(r < n_steps, b, r % n_steps, slot)
        else:
            @pl.when(b + 1 < n_tok)
            def _():
                start_chunk(True, b + 1, r - n_chunks, slot)

    for r in range(ahead):
        start_chunk(True, 0, r, r % n_slots)

    def token(b, carry):
        for r in range(n_chunks):
            slot = r % n_slots
            wait_chunk(r < n_steps, slot)
            start_ahead(b, r + ahead)
            if r < n_steps:
                key_chunk(b, r, slot)
            else:
                value_chunk(b, r - n_steps, slot)
        return carry

    lax.fori_loop(0, n_tok, token, 0)


def _fox_sample(page_table, fq, logf, fk, fv, ckt, cvt, clft, pps):
    t, n_pages = page_table.shape
    page = ckt.shape[3]
    assert page == LANES
    n_steps = n_pages // pps
    n_slots = _DECODE_SLOTS if (2 * n_steps) % _DECODE_SLOTS == 0 and _DECODE_SLOTS - 1 <= n_steps else 2
    full = lambda w: pl.BlockSpec((t, w), lambda i, pt: (0, 0))
    hbm = pl.BlockSpec(memory_space=pl.ANY)
    hb = (FOX_HEADS, LANES)
    grid_spec = pltpu.PrefetchScalarGridSpec(
        num_scalar_prefetch=1,
        grid=(1,),
        in_specs=[full(FOX_W), full(FOX_HEADS), full(FOX_W), full(FOX_W), hbm, hbm, hbm],
        out_specs=full(FOX_W),
        scratch_shapes=[pltpu.VMEM((n_slots, pps, FOX_HEADS, FOX_HD, LANES), F32),
                        pltpu.VMEM((n_slots, pps) + hb, F32),
                        pltpu.SemaphoreType.DMA((n_slots,)), pltpu.SemaphoreType.DMA((n_slots,)),
                        pltpu.VMEM((FOX_W, LANES), F32),
                        pltpu.VMEM((pps, FOX_HEADS, 8, LANES), F32),
                        pltpu.VMEM((pps,) + hb, F32),
                        pltpu.VMEM((n_pages,) + hb, F32),
                        pltpu.VMEM(hb, F32), pltpu.VMEM(hb, F32), pltpu.VMEM(hb, F32), pltpu.VMEM(hb, F32),
                        pltpu.VMEM((FOX_HEADS, FOX_HD, LANES), F32), pltpu.VMEM(hb, F32)],
    )
    return pl.pallas_call(
        functools.partial(_fox_sample_kernel, pps=pps, n_steps=n_steps, n_pages=n_pages, n_tok=t,
                          n_slots=n_slots),
        grid_spec=grid_spec,
        out_shape=jax.ShapeDtypeStruct((t, FOX_W), F32),
        compiler_params=_params(("arbitrary",)),
        name="fox_sample",
    )(page_table, fq, logf, fk, fv, ckt, clft, cvt)


def _memkv_kernel(x_ref, g_ref, w_ref, mk_ref, mv_ref):
    hn = _rms(x_ref[...], g_ref[...]).astype(BF16)
    mk_ref[...] = _dot(hn, w_ref[:, :XA_W])
    mv_ref[...] = _dot(hn, w_ref[:, XA_W:])


def _memkv(mem, g_mem, w_kv, tm):
    rows = mem.shape[0]
    return pl.pallas_call(
        _memkv_kernel,
        grid=(rows // tm,),
        in_specs=[pl.BlockSpec((tm, D_MODEL), lambda i: (i, 0)),
                  pl.BlockSpec((1, D_MODEL), lambda i: (0, 0)),
                  pl.BlockSpec((D_MODEL, 2 * XA_W), lambda i: (0, 0))],
        out_specs=[pl.BlockSpec((tm, XA_W), lambda i: (i, 0))] * 2,
        out_shape=[jax.ShapeDtypeStruct((rows, XA_W), F32)] * 2,
        compiler_params=_params(("arbitrary",)),
        name="memkv",
    )(mem, g_mem.reshape(1, D_MODEL), w_kv)


_MXU_DIM = 256
_FF_CHUNK = 4 * _MXU_DIM
_FF_CHUNKS = tuple((c, min(_FF_CHUNK, D_FF - c)) for c in range(0, D_FF, _FF_CHUNK))


def _post_kernel(x_ref, gates_ref, of_ref, or_ref, ox_ref, wf_ref, wr_ref, wx_ref, wo_ref, gffn_ref,
                 wgu_ref, wd_ref, gfin_ref, y_ref):
    mix = None
    for idx, (o_ref, w_ref) in enumerate(((of_ref, wf_ref), (or_ref, wr_ref), (ox_ref, wx_ref))):
        gate = _sigmoid(gates_ref[:, idx * D_MODEL:(idx + 1) * D_MODEL])
        term = gate * _dot(o_ref[...].astype(BF16), w_ref[...])
        mix = term if mix is None else mix + term
    x = x_ref[...] + _dot(mix.astype(BF16), wo_ref[...])
    hb = _rms(x, gffn_ref[...]).astype(BF16)
    ffn = jnp.zeros_like(x)
    for c, w in _FF_CHUNKS:
        u_gate = _dot(hb, wgu_ref[:, c:c + w])
        u_up = _dot(hb, wgu_ref[:, D_FF + c:D_FF + c + w])
        act = (u_gate * _sigmoid(u_gate) * u_up).astype(BF16)
        ffn = ffn + _dot(act, wd_ref[c:c + w, :])
    y_ref[...] = _rms(x + ffn, gfin_ref[...])


def _post(x, gates, o_fox, o_ret, o_xa, wf, wr, wx, wo, g_ffn, wgu, wd, g_final, tm):
    rows = x.shape[0]
    row = lambda w: pl.BlockSpec((tm, w), lambda i: (i, 0))
    const = lambda a, b: pl.BlockSpec((a, b), lambda i: (0, 0))
    return pl.pallas_call(
        _post_kernel,
        grid=(rows // tm,),
        in_specs=[row(D_MODEL), row(3 * D_MODEL), row(FOX_W), row(RET_V_W), row(XA_W),
                  const(FOX_W, D_MODEL), const(RET_V_W, D_MODEL), const(XA_W, D_MODEL),
                  const(D_MODEL, D_MODEL), const(1, D_MODEL), const(D_MODEL, 2 * D_FF),
                  const(D_FF, D_MODEL), const(1, D_MODEL)],
        out_specs=row(D_MODEL),
        out_shape=jax.ShapeDtypeStruct((rows, D_MODEL), F32),
        compiler_params=_params(("arbitrary",)),
        name="post",
    )(x, gates, o_fox, o_ret, o_xa, wf, wr, wx, wo, g_ffn.reshape(1, D_MODEL), wgu, wd,
      g_final.reshape(1, D_MODEL))


def _rotary_tables(pos):
    half = RET_DK // 2
    inv = ROPE_BASE ** (-jnp.arange(half, dtype=F32) / half)
    ang = pos.astype(F32)[:, None] * inv[None, :]
    cos, sin = jnp.cos(ang), jnp.sin(ang)
    zero = jnp.zeros_like(sin)
    tile = lambda a, b: jnp.tile(jnp.concatenate([a, b], axis=1), (1, RET_HEADS))
    return tile(cos, cos), tile(-sin, zero), tile(zero, sin)


def _pick_tile(n, pref):
    t = min(n, pref)
    while n % t:
        t //= 2
    return t


def kernel(x_prompt, x_sample, mem_prompt, cache_fox_k, cache_fox_v, cache_fox_logf, state_ret, cache_mem_k, cache_mem_v, page_table, g_attn, w_in, b_f, g_ret, w_br_fox, w_br_ret, w_br_xa, w_o, g_ffn, w_gu, w_down, g_mem, w_mem_kv, g_final):
    bp, sp, _ = x_prompt.shape
    bs, ts, _ = x_sample.shape
    depth = w_in.shape[0]
    assert depth == 1 and ts == 1
    n_pages, page = page_table.shape[1], cache_fox_k.shape[2]
    past = n_pages * page
    n_mem = mem_prompt.shape[1]
    l = 0

    flog0 = 3 * FOX_W
    w_l = w_in[l]
    w_main = jnp.concatenate([w_l[:, :flog0], w_l[:, flog0 + FOX_HEADS:]], axis=1).astype(BF16)
    w_kvt = w_l[:, FOX_W:flog0].T.astype(BF16)
    w_flt = w_l[:, flog0:flog0 + FOX_HEADS].T.astype(BF16)
    wf, wr, wx = w_br_fox[l].astype(BF16), w_br_ret[l].astype(BF16), w_br_xa[l].astype(BF16)
    wo, wgu, wd = w_o[l].astype(BF16), w_gu[l].astype(BF16), w_down[l].astype(BF16)
    w_kv = w_mem_kv[l].astype(BF16)

    def token_major(a_t, n_heads, head_w):
        b, _, s = a_t.shape
        return a_t.reshape(b, n_heads, head_w, s).transpose(0, 3, 1, 2)[None]

    mk_p, mv_p = _memkv(mem_prompt.reshape(bp * n_mem, D_MODEL), g_mem[l], w_kv, _pick_tile(bp * n_mem, 256))
    mk_p = mk_p.reshape(bp, n_mem, XA_W)
    mv_p = mv_p.reshape(bp, n_mem, XA_W)
    tm = _pick_tile(sp, 256)
    cos, s1, s2 = _rotary_tables(jnp.arange(sp))
    (gates, fqb, fkt, fvt, fktb, fvb, lft, ct, rq, rk, rv, rg, xq) = _inproj(
        x_prompt, g_attn[l], w_main, w_kvt, w_flt, b_f[l], cos, s1, s2, tm, False)
    o_fox = _fox_prompt(fqb, fktb, fvb, ct, _pick_tile(sp // 2, 512))
    o_ret, st_p = _ret_prompt(rq, rk, rv, rg, g_ret[l])
    o_xa = _xattn_prompt(xq, mk_p, mv_p, _pick_tile(sp, 1024))
    rows = bp * sp
    y_p = _post(x_prompt.reshape(rows, D_MODEL), gates.reshape(rows, 3 * D_MODEL), o_fox.reshape(rows, FOX_W),
                o_ret.reshape(rows, RET_V_W), o_xa.reshape(rows, XA_W), wf, wr, wx, wo, g_ffn[l], wgu, wd,
                g_final, _pick_tile(rows, 256)).reshape(bp, sp, D_MODEL)

    cos_s, s1_s, s2_s = _rotary_tables(jnp.full((bs,), past, jnp.int32))
    (gates_s, fqb_s, fkt_s, fvt_s, _, fvb_s, lft_s, _, rq_s, rk_s, rv_s, rg_s, xq_s, fkb_s) = _inproj(
        x_sample.reshape(1, bs, D_MODEL), g_attn[l], w_main, w_kvt, w_flt, b_f[l], cos_s, s1_s, s2_s, bs, True)
    two = lambda a: a.reshape(bs, a.shape[-1])
    ckt = cache_fox_k[l].transpose(0, 2, 3, 1)
    cvt = cache_fox_v[l].transpose(0, 2, 3, 1)
    clft = cache_fox_logf[l].transpose(0, 2, 1)
    o_fox_s = _fox_sample(page_table, two(fqb_s).astype(F32), lft_s[0].T, two(fkb_s).astype(F32),
                          two(fvb_s).astype(F32), ckt, cvt, clft, _pick_tile(n_pages, _DECODE_PAGES))
    o_ret_s, st_s = _ret_sample(two(rq_s), two(rk_s), two(rv_s), two(rg_s), g_ret[l], state_ret[l],
                                _pick_tile(bs, 32))
    o_xa_s = _xattn_sample(two(xq_s).reshape(bs, XA_HEADS, XA_HD),
                           cache_mem_k[l].reshape(bs, n_mem * XA_HEADS, XA_HD),
                           cache_mem_v[l].reshape(bs, n_mem * XA_HEADS, XA_HD),
                           _pick_tile(bs, 8)).reshape(bs, XA_W)
    y_s = _post(x_sample.reshape(bs, D_MODEL), two(gates_s), o_fox_s, o_ret_s, o_xa_s, wf, wr, wx, wo,
                g_ffn[l], wgu, wd, g_final, bs).reshape(bs, 1, D_MODEL)

    stack = lambda a, shape: a.reshape((1,) + shape)
    sample_major = lambda a_t, n_heads, head_w: (
        a_t.reshape(n_heads, head_w, bs).transpose(2, 0, 1).reshape(1, bs, 1, n_heads, head_w))
    return (y_p, y_s,
            token_major(fkt, FOX_HEADS, FOX_HD), token_major(fvt, FOX_HEADS, FOX_HD),
            lft.transpose(0, 2, 1)[None], stack(st_p, (bp, RET_HEADS, RET_DK, RET_DV)),
            stack(mk_p, (bp, n_mem, XA_HEADS, XA_HD)), stack(mv_p, (bp, n_mem, XA_HEADS, XA_HD)),
            sample_major(fkt_s, FOX_HEADS, FOX_HD), sample_major(fvt_s, FOX_HEADS, FOX_HD),
            lft_s[0].T.reshape(1, bs, 1, FOX_HEADS), stack(st_s, (bs, RET_HEADS, RET_DK, RET_DV)))
```
